```python
import math
import jax, jax.numpy as jnp
from jax import lax
import numpy as np

D_MODEL = 1024
BATCH = 8
SEQ = 4096
DEPTH = 1

CHUNK = 64
D_MIX = D_MODEL
HG_WIDTH = D_MIX // 2
DA_WIDTH = D_MIX - HG_WIDTH
HG_HEADS = 4
HG_DK = HG_WIDTH // HG_HEADS
HG_DV = HG_DK
DA_HEADS = 4
DA_DV = DA_WIDTH // DA_HEADS
DA_DH = DA_DV // 2
Q_BLOCK = 128
HG_Q_COLS = HG_HEADS * HG_DK
HG_F_COLS = HG_HEADS * HG_DK
HG_I_COLS = HG_HEADS * HG_DV
HG_G_COLS = HG_HEADS * HG_DV
DA_Q_COLS = DA_HEADS * 2 * DA_DH
DA_K_COLS = DA_HEADS * 2 * DA_DH
DA_V_COLS = DA_HEADS * DA_DV
D_IN = HG_Q_COLS + HG_F_COLS + HG_I_COLS + HG_G_COLS + DA_Q_COLS + DA_K_COLS + DA_V_COLS
D_CAT = HG_HEADS * HG_DV + DA_HEADS * DA_DV
PEER_HEADS = 8
PEER_N_KEYS = 128
PEER_N_EXPERTS = PEER_N_KEYS * PEER_N_KEYS
PEER_D_KEY = 256
PEER_HALF = PEER_D_KEY // 2
PEER_TOPK = 16
TOKEN_BLOCK = 128
RMS_EPS = 1e-6

kernel_name = "hymba_hgrn2_diffattn_peer_block"


def rmsnorm(x, w):
    xf = x.astype(jnp.float32)
    y = xf * lax.rsqrt(jnp.mean(xf * xf, axis=-1, keepdims=True) + RMS_EPS)
    return (y * w.astype(jnp.float32)).astype(x.dtype)


def rmsnorm_f32(x, w):
    y = x * lax.rsqrt(jnp.mean(x * x, axis=-1, keepdims=True) + RMS_EPS)
    return y * w.astype(jnp.float32)


def lambda_init(layer_idx):
    return 0.8 - 0.6 * math.exp(-0.3 * layer_idx)


def hgrn2_group(q, f_logit, inp, gate, lb, norm_w):
    B, S = q.shape[0], q.shape[1]
    n_chunks = S // CHUNK
    lbf = lb.astype(jnp.float32)
    f = lbf + (1.0 - lbf) * jax.nn.sigmoid(f_logit.astype(jnp.float32))
    k = 1.0 - f
    log_f = jnp.log(f)
    qf = q.astype(jnp.float32) * (HG_DK ** -0.5)
    vf = inp.astype(jnp.float32)

    def to_chunks(t, d):
        return t.reshape(B, n_chunks, CHUNK, HG_HEADS, d).transpose(1, 0, 3, 2, 4)

    qc, kc, ac = to_chunks(qf, HG_DK), to_chunks(k, HG_DK), to_chunks(log_f, HG_DK)
    vc = to_chunks(vf, HG_DV)
    tri = jnp.tril(jnp.ones((CHUNK, CHUNK), dtype=bool))

    def step(state, xs):
        qb, kb, vb, ab = xs
        b = jnp.cumsum(ab, axis=2)
        diff = b[:, :, :, None, :] - b[:, :, None, :, :]
        decay = jnp.exp(jnp.where(tri[:, :, None], diff, -jnp.inf))
        scores = jnp.einsum('bhtd,bhsd,bhtsd->bhts', qb, kb, decay)
        o = jnp.einsum('bhts,bhse->bhte', scores, vb) \
            + jnp.einsum('bhtd,bhde->bhte', qb * jnp.exp(b), state)
        b_last = b[:, :, -1]
        state = jnp.exp(b_last)[..., None] * state \
            + jnp.einsum('bhsd,bhse->bhde', kb * jnp.exp(b_last[:, :, None, :] - b), vb)
        return state, o

    s0 = jnp.zeros((B, HG_HEADS, HG_DK, HG_DV), jnp.float32)
    _, oc = lax.scan(step, s0, (qc, kc, vc, ac))
    o = oc.transpose(1, 0, 3, 2, 4).reshape(B, S, HG_HEADS, HG_DV)
    g = gate.astype(jnp.float32).reshape(B, S, HG_HEADS, HG_DV)
    o = rmsnorm_f32(o, norm_w) * jax.nn.silu(g)
    return o.reshape(B, S, HG_HEADS * HG_DV)


def diff_attention_group(q, k, v, lam_params, norm_w, layer_idx):
    B, S = q.shape[0], q.shape[1]
    n_blocks = S // Q_BLOCK
    lam_init = lambda_init(layer_idx)
    lp = lam_params.astype(jnp.float32)
    lam = jnp.exp(jnp.sum(lp[0] * lp[1])) - jnp.exp(jnp.sum(lp[2] * lp[3])) + lam_init
    scale = DA_DH ** -0.5
    qf = q.astype(jnp.float32).reshape(B, n_blocks, Q_BLOCK, DA_HEADS, 2, DA_DH).transpose(1, 0, 2, 3, 4, 5)
    kf = k.astype(jnp.float32).reshape(B, S, DA_HEADS, 2, DA_DH)
    vf = v.astype(jnp.float32).reshape(B, S, DA_HEADS, DA_DV)
    key_chunk = jnp.arange(S) // CHUNK

    def one_block(args):
        qb, blk = args
        s = jnp.einsum('bqhcd,bkhcd->bhcqk', qb, kf) * scale
        q_chunk = (blk * Q_BLOCK + jnp.arange(Q_BLOCK)) // CHUNK
        mask = key_chunk[None, :] <= q_chunk[:, None]
        p = jax.nn.softmax(jnp.where(mask, s, -jnp.inf), axis=-1)
        a = p[:, :, 0] - lam * p[:, :, 1]
        return jnp.einsum('bhqk,bkhe->bqhe', a, vf)

    o = lax.map(one_block, (qf, jnp.arange(n_blocks)))
    o = o.transpose(1, 0, 2, 3, 4).reshape(B, S, DA_HEADS, DA_DV)
    o = rmsnorm_f32(o, norm_w) * (1.0 - lam_init)
    return o.reshape(B, S, DA_HEADS * DA_DV)


def peer_ffn(x, w_query, sub_keys, u, v):
    B, S, D = x.shape
    xt = x.reshape((B * S) // TOKEN_BLOCK, TOKEN_BLOCK, D)
    wq = w_query.reshape(D, PEER_HEADS, PEER_D_KEY)

    def one_block(xb):
        qh = jnp.einsum('td,dhk->thk', xb, wq)
        s1 = jnp.einsum('thk,hnk->thn', qh[..., :PEER_HALF], sub_keys[:, 0])
        s2 = jnp.einsum('thk,hnk->thn', qh[..., PEER_HALF:], sub_keys[:, 1])
        v1, i1 = lax.top_k(s1, PEER_TOPK)
        v2, i2 = lax.top_k(s2, PEER_TOPK)
        T = xb.shape[0]
        cand = (v1[..., :, None] + v2[..., None, :]).reshape(T, PEER_HEADS, PEER_TOPK * PEER_TOPK)
        cand_idx = (i1[..., :, None] * PEER_N_KEYS + i2[..., None, :]).reshape(T, PEER_HEADS, PEER_TOPK * PEER_TOPK)
        top_s, pos = lax.top_k(cand, PEER_TOPK)
        idx = jnp.take_along_axis(cand_idx, pos, axis=-1)
        g = jax.nn.softmax(top_s.astype(jnp.float32), axis=-1).astype(xb.dtype)
        ue = u[idx]
        act = jax.nn.gelu(jnp.einsum('thkd,td->thk', ue, xb))
        ve = v[idx]
        return jnp.einsum('thk,thkd->td', g * act, ve)

    out = lax.map(one_block, xt)
    return out.reshape(B, S, D)


def setup_inputs(seed: int = 0) -> dict:
    key = jax.random.key(seed)
    ks = jax.random.split(key, 16)
    f32 = jnp.float32
    nrm = lambda k, shape, s: jax.random.normal(k, shape, f32) * s
    gain = lambda k, shape: 1.0 + 0.02 * jax.random.normal(k, shape, f32)
    return {
        "x": nrm(ks[0], (BATCH, SEQ, D_MODEL), 1.0),
        "norm_mix_w": gain(ks[1], (DEPTH, D_MODEL)),
        "w_in": nrm(ks[2], (DEPTH, D_MODEL, D_IN), D_MODEL ** -0.5),
        "hgrn_lb_param": nrm(ks[3], (DEPTH + 1, HG_HEADS * HG_DK), 0.5),
        "hgrn_norm_w": gain(ks[4], (DEPTH, HG_DV)),
        "diff_lambda": nrm(ks[5], (DEPTH, 4, DA_DH), 0.1),
        "diff_norm_w": gain(ks[6], (DEPTH, DA_DV)),
        "w_out": nrm(ks[7], (DEPTH, D_CAT, D_MODEL), D_CAT ** -0.5),
        "norm_ffn_w": gain(ks[8], (DEPTH, D_MODEL)),
        "peer_w_query": nrm(ks[9], (DEPTH, D_MODEL, PEER_HEADS * PEER_D_KEY), D_MODEL ** -0.5),
        "peer_sub_keys": nrm(ks[10], (DEPTH, PEER_HEADS, 2, PEER_N_KEYS, PEER_HALF), PEER_HALF ** -0.5),
        "peer_u": nrm(ks[11], (DEPTH, PEER_N_EXPERTS, D_MODEL), D_MODEL ** -0.5),
        "peer_v": nrm(ks[12], (DEPTH, PEER_N_EXPERTS, D_MODEL), 0.5),
        "norm_final_w": gain(ks[13], (D_MODEL,)),
    }


def reference(x, norm_mix_w, w_in, hgrn_lb_param, hgrn_norm_w, diff_lambda, diff_norm_w,
              w_out, norm_ffn_w, peer_w_query, peer_sub_keys, peer_u, peer_v, norm_final_w):
    lower_bounds = jnp.cumsum(jax.nn.softmax(hgrn_lb_param.astype(jnp.float32), axis=0), axis=0)
    c1 = HG_Q_COLS
    c2 = c1 + HG_F_COLS
    c3 = c2 + HG_I_COLS
    c4 = c3 + HG_G_COLS
    c5 = c4 + DA_Q_COLS
    c6 = c5 + DA_K_COLS
    h = x
    for l in range(DEPTH):
        n1 = rmsnorm(h, norm_mix_w[l])
        proj = jnp.einsum('bsd,de->bse', n1, w_in[l])
        hg_out = hgrn2_group(proj[..., :c1], proj[..., c1:c2], proj[..., c2:c3], proj[..., c3:c4],
                             lower_bounds[l], hgrn_norm_w[l])
        da_out = diff_attention_group(proj[..., c4:c5], proj[..., c5:c6], proj[..., c6:],
                                      diff_lambda[l], diff_norm_w[l], l)
        mixed = jnp.concatenate([hg_out, da_out], axis=-1).astype(h.dtype)
        h = h + jnp.einsum('bse,ed->bsd', mixed, w_out[l])
        n2 = rmsnorm(h, norm_ffn_w[l])
        h = h + peer_ffn(n2, peer_w_query[l], peer_sub_keys[l], peer_u[l], peer_v[l]).astype(h.dtype)
    return rmsnorm(h, norm_final_w)
```

```python
import math
import jax, jax.numpy as jnp
from jax import lax
from jax.experimental import pallas as pl

D_MODEL = 1024
BATCH = 8
SEQ = 4096
CHUNK = 64
HG_HEADS = 4
HG_DK = 128
HG_DV = 128
DA_HEADS = 4
DA_DV = 128
DA_DH = 64
Q_BLOCK = 128
PEER_HEADS = 8
PEER_N_KEYS = 128
PEER_D_KEY = 256
PEER_HALF = 128
PEER_TOPK = 16
TOKEN_BLOCK = 128
RMS_EPS = 1e-6


def _rmsnorm(x, w):
    y = x * lax.rsqrt(jnp.mean(x * x, axis=-1, keepdims=True) + RMS_EPS)
    return y * w


def _hgrn2(q, f_logit, inp, gate, lb, norm_w):
    B, S = q.shape[0], q.shape[1]
    n_chunks = S // CHUNK
    f = lb + (1.0 - lb) * jax.nn.sigmoid(f_logit)
    k = 1.0 - f
    log_f = jnp.log(f)
    qf = q * (HG_DK ** -0.5)

    def to_chunks(t, d):
        return t.reshape(B, n_chunks, CHUNK, HG_HEADS, d).transpose(1, 0, 3, 2, 4)

    qc, kc, ac = to_chunks(qf, HG_DK), to_chunks(k, HG_DK), to_chunks(log_f, HG_DK)
    vc = to_chunks(inp, HG_DV)
    tri = jnp.tril(jnp.ones((CHUNK, CHUNK), dtype=bool))

    def step(state, xs):
        qb, kb, vb, ab = xs
        b = jnp.cumsum(ab, axis=2)
        diff = b[:, :, :, None, :] - b[:, :, None, :, :]
        decay = jnp.exp(jnp.where(tri[:, :, None], diff, -jnp.inf))
        scores = jnp.einsum('bhtd,bhsd,bhtsd->bhts', qb, kb, decay)
        o = jnp.einsum('bhts,bhse->bhte', scores, vb) \
            + jnp.einsum('bhtd,bhde->bhte', qb * jnp.exp(b), state)
        b_last = b[:, :, -1]
        state = jnp.exp(b_last)[..., None] * state \
            + jnp.einsum('bhsd,bhse->bhde', kb * jnp.exp(b_last[:, :, None, :] - b), vb)
        return state, o

    s0 = jnp.zeros((B, HG_HEADS, HG_DK, HG_DV), jnp.float32)
    _, oc = lax.scan(step, s0, (qc, kc, vc, ac))
    o = oc.transpose(1, 0, 3, 2, 4).reshape(B, S, HG_HEADS, HG_DV)
    g = gate.reshape(B, S, HG_HEADS, HG_DV)
    o = _rmsnorm(o, norm_w) * jax.nn.silu(g)
    return o.reshape(B, S, HG_HEADS * HG_DV)


def _diff_attn(q, k, v, lam_params, norm_w):
    B, S = q.shape[0], q.shape[1]
    n_blocks = S // Q_BLOCK
    lam_init = 0.8 - 0.6 * math.exp(0.0)
    lp = lam_params
    lam = jnp.exp(jnp.sum(lp[0] * lp[1])) - jnp.exp(jnp.sum(lp[2] * lp[3])) + lam_init
    scale = DA_DH ** -0.5
    qf = q.reshape(B, n_blocks, Q_BLOCK, DA_HEADS, 2, DA_DH).transpose(1, 0, 2, 3, 4, 5)
    kf = k.reshape(B, S, DA_HEADS, 2, DA_DH)
    vf = v.reshape(B, S, DA_HEADS, DA_DV)
    key_chunk = jnp.arange(S) // CHUNK

    def one_block(args):
        qb, blk = args
        s = jnp.einsum('bqhcd,bkhcd->bhcqk', qb, kf) * scale
        q_chunk = (blk * Q_BLOCK + jnp.arange(Q_BLOCK)) // CHUNK
        mask = key_chunk[None, :] <= q_chunk[:, None]
        p = jax.nn.softmax(jnp.where(mask, s, -jnp.inf), axis=-1)
        a = p[:, :, 0] - lam * p[:, :, 1]
        return jnp.einsum('bhqk,bkhe->bqhe', a, vf)

    o = lax.map(one_block, (qf, jnp.arange(n_blocks)))
    o = o.transpose(1, 0, 2, 3, 4).reshape(B, S, DA_HEADS, DA_DV)
    o = _rmsnorm(o, norm_w) * (1.0 - lam_init)
    return o.reshape(B, S, DA_HEADS * DA_DV)


def _peer(x, w_query, sub_keys, u, v):
    B, S, D = x.shape
    xt = x.reshape((B * S) // TOKEN_BLOCK, TOKEN_BLOCK, D)
    wq = w_query.reshape(D, PEER_HEADS, PEER_D_KEY)

    def one_block(xb):
        qh = jnp.einsum('td,dhk->thk', xb, wq)
        s1 = jnp.einsum('thk,hnk->thn', qh[..., :PEER_HALF], sub_keys[:, 0])
        s2 = jnp.einsum('thk,hnk->thn', qh[..., PEER_HALF:], sub_keys[:, 1])
        v1, i1 = lax.top_k(s1, PEER_TOPK)
        v2, i2 = lax.top_k(s2, PEER_TOPK)
        T = xb.shape[0]
        cand = (v1[..., :, None] + v2[..., None, :]).reshape(T, PEER_HEADS, PEER_TOPK * PEER_TOPK)
        cand_idx = (i1[..., :, None] * PEER_N_KEYS + i2[..., None, :]).reshape(T, PEER_HEADS, PEER_TOPK * PEER_TOPK)
        top_s, pos = lax.top_k(cand, PEER_TOPK)
        idx = jnp.take_along_axis(cand_idx, pos, axis=-1)
        g = jax.nn.softmax(top_s, axis=-1)
        ue = u[idx]
        act = jax.nn.gelu(jnp.einsum('thkd,td->thk', ue, xb))
        ve = v[idx]
        return jnp.einsum('thk,thkd->td', g * act, ve)

    out = lax.map(one_block, xt)
    return out.reshape(B, S, D)


def _final_norm_kernel(h_ref, w_ref, o_ref):
    h = h_ref[...]
    y = h * lax.rsqrt(jnp.mean(h * h, axis=-1, keepdims=True) + RMS_EPS)
    o_ref[...] = y * w_ref[...]


def kernel(x, norm_mix_w, w_in, hgrn_lb_param, hgrn_norm_w, diff_lambda, diff_norm_w,
           w_out, norm_ffn_w, peer_w_query, peer_sub_keys, peer_u, peer_v, norm_final_w):
    lower_bounds = jnp.cumsum(jax.nn.softmax(hgrn_lb_param, axis=0), axis=0)
    c1, c2, c3, c4, c5, c6 = 512, 1024, 1536, 2048, 2560, 3072
    n1 = _rmsnorm(x, norm_mix_w[0])
    proj = jnp.einsum('bsd,de->bse', n1, w_in[0])
    hg_out = _hgrn2(proj[..., :c1], proj[..., c1:c2], proj[..., c2:c3], proj[..., c3:c4],
                    lower_bounds[0], hgrn_norm_w[0])
    da_out = _diff_attn(proj[..., c4:c5], proj[..., c5:c6], proj[..., c6:],
                        diff_lambda[0], diff_norm_w[0])
    mixed = jnp.concatenate([hg_out, da_out], axis=-1)
    h = x + jnp.einsum('bse,ed->bsd', mixed, w_out[0])
    n2 = _rmsnorm(h, norm_ffn_w[0])
    h = h + _peer(n2, peer_w_query[0], peer_sub_keys[0], peer_u[0], peer_v[0])

    n = BATCH * SEQ
    tm = 1024
    out = pl.pallas_call(
        _final_norm_kernel,
        grid=(n // tm,),
        in_specs=[pl.BlockSpec((tm, D_MODEL), lambda i: (i, 0)),
                  pl.BlockSpec((1, D_MODEL), lambda i: (0, 0))],
        out_specs=pl.BlockSpec((tm, D_MODEL), lambda i: (i, 0)),
        out_shape=jax.ShapeDtypeStruct((n, D_MODEL), jnp.float32),
    )(h.reshape(n, D_MODEL), norm_final_w.reshape(1, D_MODEL))
    return out.reshape(BATCH, SEQ, D_MODEL)
```

```python
import math
import jax, jax.numpy as jnp
from jax import lax
from jax.experimental import pallas as pl
from jax.experimental.pallas import tpu as pltpu

D_MODEL = 1024
BATCH = 8
SEQ = 4096
CHUNK = 64
HG_HEADS = 4
HG_DK = 128
HG_DV = 128
DA_HEADS = 4
DA_DV = 128
DA_DH = 64
Q_BLOCK = 128
PEER_HEADS = 8
PEER_N_KEYS = 128
PEER_D_KEY = 256
PEER_HALF = 128
PEER_TOPK = 16
TOKEN_BLOCK = 128
RMS_EPS = 1e-6


def _rmsnorm(x, w):
    y = x * lax.rsqrt(jnp.mean(x * x, axis=-1, keepdims=True) + RMS_EPS)
    return y * w


def _hgrn2(q, f_logit, inp, gate, lb, norm_w):
    B, S = q.shape[0], q.shape[1]
    n_chunks = S // CHUNK
    f = lb + (1.0 - lb) * jax.nn.sigmoid(f_logit)
    k = 1.0 - f
    log_f = jnp.log(f)
    qf = q * (HG_DK ** -0.5)

    def to_chunks(t, d):
        return t.reshape(B, n_chunks, CHUNK, HG_HEADS, d).transpose(1, 0, 3, 2, 4)

    qc, kc, ac = to_chunks(qf, HG_DK), to_chunks(k, HG_DK), to_chunks(log_f, HG_DK)
    vc = to_chunks(inp, HG_DV)
    tri = jnp.tril(jnp.ones((CHUNK, CHUNK), dtype=bool))

    def step(state, xs):
        qb, kb, vb, ab = xs
        b = jnp.cumsum(ab, axis=2)
        diff = b[:, :, :, None, :] - b[:, :, None, :, :]
        decay = jnp.exp(jnp.where(tri[:, :, None], diff, -jnp.inf))
        scores = jnp.einsum('bhtd,bhsd,bhtsd->bhts', qb, kb, decay)
        o = jnp.einsum('bhts,bhse->bhte', scores, vb) \
            + jnp.einsum('bhtd,bhde->bhte', qb * jnp.exp(b), state)
        b_last = b[:, :, -1]
        state = jnp.exp(b_last)[..., None] * state \
            + jnp.einsum('bhsd,bhse->bhde', kb * jnp.exp(b_last[:, :, None, :] - b), vb)
        return state, o

    s0 = jnp.zeros((B, HG_HEADS, HG_DK, HG_DV), jnp.float32)
    _, oc = lax.scan(step, s0, (qc, kc, vc, ac))
    o = oc.transpose(1, 0, 3, 2, 4).reshape(B, S, HG_HEADS, HG_DV)
    g = gate.reshape(B, S, HG_HEADS, HG_DV)
    o = _rmsnorm(o, norm_w) * jax.nn.silu(g)
    return o.reshape(B, S, HG_HEADS * HG_DV)


def _diff_attn(q, k, v, lam_params, norm_w):
    B, S = q.shape[0], q.shape[1]
    n_blocks = S // Q_BLOCK
    lam_init = 0.8 - 0.6 * math.exp(0.0)
    lp = lam_params
    lam = jnp.exp(jnp.sum(lp[0] * lp[1])) - jnp.exp(jnp.sum(lp[2] * lp[3])) + lam_init
    scale = DA_DH ** -0.5
    qf = q.reshape(B, n_blocks, Q_BLOCK, DA_HEADS, 2, DA_DH).transpose(1, 0, 2, 3, 4, 5)
    kf = k.reshape(B, S, DA_HEADS, 2, DA_DH)
    vf = v.reshape(B, S, DA_HEADS, DA_DV)
    key_chunk = jnp.arange(S) // CHUNK

    def one_block(args):
        qb, blk = args
        s = jnp.einsum('bqhcd,bkhcd->bhcqk', qb, kf) * scale
        q_chunk = (blk * Q_BLOCK + jnp.arange(Q_BLOCK)) // CHUNK
        mask = key_chunk[None, :] <= q_chunk[:, None]
        p = jax.nn.softmax(jnp.where(mask, s, -jnp.inf), axis=-1)
        a = p[:, :, 0] - lam * p[:, :, 1]
        return jnp.einsum('bhqk,bkhe->bqhe', a, vf)

    o = lax.map(one_block, (qf, jnp.arange(n_blocks)))
    o = o.transpose(1, 0, 2, 3, 4).reshape(B, S, DA_HEADS, DA_DV)
    o = _rmsnorm(o, norm_w) * (1.0 - lam_init)
    return o.reshape(B, S, DA_HEADS * DA_DV)


PEER_N_EXPERTS = PEER_N_KEYS * PEER_N_KEYS
PEER_SEL = PEER_HEADS * PEER_TOPK
SLAB = D_MODEL // 2 // 128
HALF_D = D_MODEL // 2
_CAND_NB = [PEER_TOPK // (a + 1) for a in range(PEER_TOPK)]
_CAND_ROWS = 56


def _topk_rows(s, k):
    n = s.shape[0]
    iota = lax.broadcasted_iota(jnp.int32, s.shape, 0)
    vals, idxs = [], []
    for _ in range(k):
        m = jnp.max(s, axis=0, keepdims=True)
        idx = jnp.min(jnp.where(s == m, iota, n), axis=0, keepdims=True)
        s = jnp.where(iota == idx, -jnp.inf, s)
        vals.append(m)
        idxs.append(idx)
    return jnp.concatenate(vals, axis=0), jnp.concatenate(idxs, axis=0)


def _peer_topk_kernel(n2_ref, wqt_ref, sk_ref, idx_ref, g_ref, qh_scr):
    qh_scr[...] = lax.dot_general(wqt_ref[...], n2_ref[...], (((1,), (1,)), ((), ())),
                                  preferred_element_type=jnp.float32)

    def head(h, carry):
        base = pl.multiple_of(h * PEER_D_KEY, PEER_D_KEY)
        q1 = qh_scr[pl.ds(base, PEER_HALF), :].astype(jnp.bfloat16)
        q2 = qh_scr[pl.ds(base + PEER_HALF, PEER_HALF), :].astype(jnp.bfloat16)
        s1 = jnp.dot(sk_ref[h, 0], q1, preferred_element_type=jnp.float32)
        s2 = jnp.dot(sk_ref[h, 1], q2, preferred_element_type=jnp.float32)
        v1, i1 = _topk_rows(s1, PEER_TOPK)
        v2, i2 = _topk_rows(s2, PEER_TOPK)
        tb = v1.shape[1]
        cand = jnp.concatenate(
            [v1[a:a + 1] + v2[:_CAND_NB[a]] for a in range(PEER_TOPK)]
            + [jnp.full((_CAND_ROWS - sum(_CAND_NB), tb), -jnp.inf, jnp.float32)], axis=0)
        cidx = jnp.concatenate(
            [i1[a:a + 1] * PEER_N_KEYS + i2[:_CAND_NB[a]] for a in range(PEER_TOPK)]
            + [jnp.zeros((_CAND_ROWS - sum(_CAND_NB), tb), jnp.int32)], axis=0)
        ts, pos = _topk_rows(cand, PEER_TOPK)
        iota = lax.broadcasted_iota(jnp.int32, cand.shape, 0)
        eid = jnp.concatenate(
            [jnp.sum(jnp.where(iota == pos[r:r + 1], cidx, 0), axis=0, keepdims=True)
             for r in range(PEER_TOPK)], axis=0)
        e = jnp.exp(ts - ts[0:1])
        g = e / jnp.sum(e, axis=0, keepdims=True)
        row = pl.multiple_of(h * PEER_TOPK, PEER_TOPK)
        idx_ref[pl.ds(row, PEER_TOPK), :] = eid
        g_ref[pl.ds(row, PEER_TOPK), :] = g
        return carry

    lax.fori_loop(0, PEER_HEADS, head, 0)


def _peer_topk(n2b, wqt, sk, tb=256):
    n = n2b.shape[0]
    return pl.pallas_call(
        _peer_topk_kernel,
        grid=(n // tb,),
        in_specs=[pl.BlockSpec((tb, D_MODEL), lambda i: (i, 0)),
                  pl.BlockSpec(wqt.shape, lambda i: (0, 0)),
                  pl.BlockSpec(sk.shape, lambda i: (0, 0, 0, 0))],
        out_specs=[pl.BlockSpec((PEER_SEL, tb), lambda i: (0, i)),
                   pl.BlockSpec((PEER_SEL, tb), lambda i: (0, i))],
        out_shape=[jax.ShapeDtypeStruct((PEER_SEL, n), jnp.int32),
                   jax.ShapeDtypeStruct((PEER_SEL, n), jnp.float32)],
        scratch_shapes=[pltpu.VMEM((PEER_HEADS * PEER_D_KEY, tb), jnp.float32)],
        compiler_params=pltpu.CompilerParams(dimension_semantics=("arbitrary",),
                                             vmem_limit_bytes=40 * 1024 * 1024),
        name="peer_topk",
    )(n2b, wqt, sk)


def _pack_table(t):
    e = t.shape[0]
    tb = t.astype(jnp.bfloat16).reshape(e, 2, SLAB, 128)
    bits = lax.bitcast_convert_type(tb, jnp.uint16).astype(jnp.uint32)
    word = bits[:, 0] | (bits[:, 1] << 16)
    return lax.bitcast_convert_type(word, jnp.int32).reshape(e * SLAB, 128)


def _gather_rows(idx_ref, base, tab_ref, tile):
    for mi in range(PEER_SEL):
        i = pl.multiple_of(idx_ref[base + mi], SLAB)
        tile[pl.ds(SLAB * mi, SLAB), :] = tab_ref[pl.ds(i, SLAB), :]


def _gathered_matrix(tile):
    return jnp.concatenate(
        [pltpu.bitcast(tile[pl.ds(j, PEER_SEL, stride=SLAB), :], jnp.bfloat16) for j in range(SLAB)],
        axis=-1)


GROUP = 8


def _peer_u_kernel(idx_ref, tab_ref, x_ref, g_ref, w_ref, tile_a, tile_b, r_scr):
    tb = x_ref.shape[0]

    def group(gi, carry):
        r0 = pl.multiple_of(gi * GROUP, GROUP)
        xg = x_ref[pl.ds(r0, GROUP), :]
        rs = []
        for k in range(GROUP):
            tile = tile_a if k % 2 == 0 else tile_b
            _gather_rows(idx_ref, (r0 + k) * PEER_SEL, tab_ref, tile)
            x2 = jnp.concatenate([xg[k:k + 1, :HALF_D], xg[k:k + 1, HALF_D:]], axis=0).astype(jnp.bfloat16)
            rs.append(lax.dot_general(x2, _gathered_matrix(tile), (((1,), (1,)), ((), ())),
                                      preferred_element_type=jnp.float32))
        r_scr[pl.ds(pl.multiple_of(gi * 2 * GROUP, 2 * GROUP), 2 * GROUP), :] = jnp.concatenate(rs, axis=0)
        return carry

    lax.fori_loop(0, tb // GROUP, group, 0)

    e = r_scr[...]
    nl = e.shape[1]
    row = lax.broadcasted_iota(jnp.int32, e.shape, 0)
    lane = lax.broadcasted_iota(jnp.int32, e.shape, 1)
    nxt = pltpu.roll(pltpu.roll(e, e.shape[0] - 1, axis=0), nl - 1, axis=1)
    prv = pltpu.roll(pltpu.roll(e, 1, axis=0), 1, axis=1)
    act = e + jnp.where(row % 2 == 0, nxt, prv)
    w = jnp.where(row % 2 == lane % 2, g_ref[...] * jax.nn.gelu(act), 0.0)
    w_ref[...] = w.astype(jnp.bfloat16)


def _peer_v_kernel(idx_ref, tab_ref, w_ref, h_ref, nw_ref, o_ref, tile_a, tile_b, r_scr):
    tb = h_ref.shape[0]

    def group(gi, carry):
        r0 = pl.multiple_of(gi * GROUP, GROUP)
        wg = w_ref[pl.ds(pl.multiple_of(gi * 2 * GROUP, 2 * GROUP), 2 * GROUP), :]
        rs = []
        for k in range(GROUP):
            tile = tile_a if k % 2 == 0 else tile_b
            _gather_rows(idx_ref, (r0 + k) * PEER_SEL, tab_ref, tile)
            rs.append(jnp.dot(wg[2 * k:2 * k + 2], _gathered_matrix(tile),
                              preferred_element_type=jnp.float32))
        r_scr[pl.ds(r0, GROUP), :HALF_D] = jnp.concatenate([r[0:1] for r in rs], axis=0)
        r_scr[pl.ds(r0, GROUP), HALF_D:] = jnp.concatenate([r[1:2] for r in rs], axis=0)
        return carry

    lax.fori_loop(0, tb // GROUP, group, 0)

    y = h_ref[...] + r_scr[...]
    y = y * lax.rsqrt(jnp.mean(y * y, axis=-1, keepdims=True) + RMS_EPS)
    o_ref[...] = y * nw_ref[...]


_PEER_VMEM = 56 * 1024 * 1024


def _peer_u(idx_flat, tab, x, g2, tb=64):
    n = x.shape[0]
    return pl.pallas_call(
        _peer_u_kernel,
        grid=(n // tb,),
        in_specs=[pl.BlockSpec((tb * PEER_SEL,), lambda i: (i,), memory_space=pltpu.SMEM),
                  pl.BlockSpec(memory_space=pltpu.VMEM),
                  pl.BlockSpec((tb, D_MODEL), lambda i: (i, 0)),
                  pl.BlockSpec((2 * tb, 2 * PEER_SEL), lambda i: (i, 0))],
        out_specs=pl.BlockSpec((2 * tb, 2 * PEER_SEL), lambda i: (i, 0)),
        out_shape=jax.ShapeDtypeStruct((2 * n, 2 * PEER_SEL), jnp.bfloat16),
        scratch_shapes=[pltpu.VMEM((PEER_SEL * SLAB, 128), jnp.int32),
                        pltpu.VMEM((PEER_SEL * SLAB, 128), jnp.int32),
                        pltpu.VMEM((2 * tb, 2 * PEER_SEL), jnp.float32)],
        compiler_params=pltpu.CompilerParams(dimension_semantics=("arbitrary",),
                                             vmem_limit_bytes=_PEER_VMEM),
        name="peer_u",
    )(idx_flat, tab, x, g2)


def _peer_v(idx_flat, tab, w, h, nw, tb=64):
    n = h.shape[0]
    return pl.pallas_call(
        _peer_v_kernel,
        grid=(n // tb,),
        in_specs=[pl.BlockSpec((tb * PEER_SEL,), lambda i: (i,), memory_space=pltpu.SMEM),
                  pl.BlockSpec(memory_space=pltpu.VMEM),
                  pl.BlockSpec((2 * tb, 2 * PEER_SEL), lambda i: (i, 0)),
                  pl.BlockSpec((tb, D_MODEL), lambda i: (i, 0)),
                  pl.BlockSpec((1, D_MODEL), lambda i: (0, 0))],
        out_specs=pl.BlockSpec((tb, D_MODEL), lambda i: (i, 0)),
        out_shape=jax.ShapeDtypeStruct((n, D_MODEL), jnp.float32),
        scratch_shapes=[pltpu.VMEM((PEER_SEL * SLAB, 128), jnp.int32),
                        pltpu.VMEM((PEER_SEL * SLAB, 128), jnp.int32),
                        pltpu.VMEM((tb, D_MODEL), jnp.float32)],
        compiler_params=pltpu.CompilerParams(dimension_semantics=("arbitrary",),
                                             vmem_limit_bytes=_PEER_VMEM),
        name="peer_v",
    )(idx_flat, tab, w, h, nw)


def _peer_block(n2, h, w_query, sub_keys, u, v, norm_final_w):
    n = n2.shape[0]
    wqt = w_query.T.astype(jnp.bfloat16)
    sk = sub_keys.astype(jnp.bfloat16)
    idx_t, g_t = _peer_topk(n2.astype(jnp.bfloat16), wqt, sk)
    idx_flat = (idx_t.T * SLAB).reshape(n * PEER_SEL)
    g2 = jnp.repeat(jnp.repeat(g_t.T, 2, axis=1), 2, axis=0)
    w = _peer_u(idx_flat, _pack_table(u), n2, g2)
    return _peer_v(idx_flat, _pack_table(v), w, h, norm_final_w.reshape(1, D_MODEL))


def kernel(x, norm_mix_w, w_in, hgrn_lb_param, hgrn_norm_w, diff_lambda, diff_norm_w,
           w_out, norm_ffn_w, peer_w_query, peer_sub_keys, peer_u, peer_v, norm_final_w):
    lower_bounds = jnp.cumsum(jax.nn.softmax(hgrn_lb_param, axis=0), axis=0)
    c1, c2, c3, c4, c5, c6 = 512, 1024, 1536, 2048, 2560, 3072
    n1 = _rmsnorm(x, norm_mix_w[0])
    proj = jnp.einsum('bsd,de->bse', n1, w_in[0])
    hg_out = _hgrn2(proj[..., :c1], proj[..., c1:c2], proj[..., c2:c3], proj[..., c3:c4],
                    lower_bounds[0], hgrn_norm_w[0])
    da_out = _diff_attn(proj[..., c4:c5], proj[..., c5:c6], proj[..., c6:],
                        diff_lambda[0], diff_norm_w[0])
    mixed = jnp.concatenate([hg_out, da_out], axis=-1)
    h = x + jnp.einsum('bse,ed->bsd', mixed, w_out[0])
    n2 = _rmsnorm(h, norm_ffn_w[0])
    n = BATCH * SEQ
    out = _peer_block(n2.reshape(n, D_MODEL), h.reshape(n, D_MODEL), peer_w_query[0], peer_sub_keys[0],
                      peer_u[0], peer_v[0], norm_final_w)
    return out.reshape(BATCH, SEQ, D_MODEL)
```

```python
import math
import jax, jax.numpy as jnp
from jax import lax
from jax.experimental import pallas as pl
from jax.experimental.pallas import tpu as pltpu

D_MODEL = 1024
BATCH = 8
SEQ = 4096
CHUNK = 64
HG_HEADS = 4
HG_DK = 128
HG_DV = 128
DA_HEADS = 4
DA_DV = 128
DA_DH = 64
Q_BLOCK = 128
PEER_HEADS = 8
PEER_N_KEYS = 128
PEER_D_KEY = 256
PEER_HALF = 128
PEER_TOPK = 16
TOKEN_BLOCK = 128
RMS_EPS = 1e-6


def _rmsnorm(x, w):
    y = x * lax.rsqrt(jnp.mean(x * x, axis=-1, keepdims=True) + RMS_EPS)
    return y * w


def _hgrn2(q, f_logit, inp, gate, lb, norm_w):
    B, S = q.shape[0], q.shape[1]
    n_chunks = S // CHUNK
    f = lb + (1.0 - lb) * jax.nn.sigmoid(f_logit)
    k = 1.0 - f
    log_f = jnp.log(f)
    qf = q * (HG_DK ** -0.5)

    def to_chunks(t, d):
        return t.reshape(B, n_chunks, CHUNK, HG_HEADS, d).transpose(1, 0, 3, 2, 4)

    qc, kc, ac = to_chunks(qf, HG_DK), to_chunks(k, HG_DK), to_chunks(log_f, HG_DK)
    vc = to_chunks(inp, HG_DV)
    tri = jnp.tril(jnp.ones((CHUNK, CHUNK), dtype=bool))

    def step(state, xs):
        qb, kb, vb, ab = xs
        b = jnp.cumsum(ab, axis=2)
        diff = b[:, :, :, None, :] - b[:, :, None, :, :]
        decay = jnp.exp(jnp.where(tri[:, :, None], diff, -jnp.inf))
        scores = jnp.einsum('bhtd,bhsd,bhtsd->bhts', qb, kb, decay)
        o = jnp.einsum('bhts,bhse->bhte', scores, vb) \
            + jnp.einsum('bhtd,bhde->bhte', qb * jnp.exp(b), state)
        b_last = b[:, :, -1]
        state = jnp.exp(b_last)[..., None] * state \
            + jnp.einsum('bhsd,bhse->bhde', kb * jnp.exp(b_last[:, :, None, :] - b), vb)
        return state, o

    s0 = jnp.zeros((B, HG_HEADS, HG_DK, HG_DV), jnp.float32)
    _, oc = lax.scan(step, s0, (qc, kc, vc, ac))
    o = oc.transpose(1, 0, 3, 2, 4).reshape(B, S, HG_HEADS, HG_DV)
    g = gate.reshape(B, S, HG_HEADS, HG_DV)
    o = _rmsnorm(o, norm_w) * jax.nn.silu(g)
    return o.reshape(B, S, HG_HEADS * HG_DV)


def _diff_attn(q, k, v, lam_params, norm_w):
    B, S = q.shape[0], q.shape[1]
    n_blocks = S // Q_BLOCK
    lam_init = 0.8 - 0.6 * math.exp(0.0)
    lp = lam_params
    lam = jnp.exp(jnp.sum(lp[0] * lp[1])) - jnp.exp(jnp.sum(lp[2] * lp[3])) + lam_init
    scale = DA_DH ** -0.5
    qf = q.reshape(B, n_blocks, Q_BLOCK, DA_HEADS, 2, DA_DH).transpose(1, 0, 2, 3, 4, 5)
    kf = k.reshape(B, S, DA_HEADS, 2, DA_DH)
    vf = v.reshape(B, S, DA_HEADS, DA_DV)
    key_chunk = jnp.arange(S) // CHUNK

    def one_block(args):
        qb, blk = args
        s = jnp.einsum('bqhcd,bkhcd->bhcqk', qb, kf) * scale
        q_chunk = (blk * Q_BLOCK + jnp.arange(Q_BLOCK)) // CHUNK
        mask = key_chunk[None, :] <= q_chunk[:, None]
        p = jax.nn.softmax(jnp.where(mask, s, -jnp.inf), axis=-1)
        a = p[:, :, 0] - lam * p[:, :, 1]
        return jnp.einsum('bhqk,bkhe->bqhe', a, vf)

    o = lax.map(one_block, (qf, jnp.arange(n_blocks)))
    o = o.transpose(1, 0, 2, 3, 4).reshape(B, S, DA_HEADS, DA_DV)
    o = _rmsnorm(o, norm_w) * (1.0 - lam_init)
    return o.reshape(B, S, DA_HEADS * DA_DV)


PROJ_TM = 512
PROJ_TN = 512


def _norm_proj_kernel(x_ref, nw_ref, w_ref, o_ref, n_scr):
    @pl.when(pl.program_id(1) == 0)
    def _():
        x = x_ref[...]
        y = x * lax.rsqrt(jnp.mean(x * x, axis=-1, keepdims=True) + RMS_EPS) * nw_ref[...]
        n_scr[...] = y.astype(jnp.bfloat16)

    o_ref[...] = jnp.dot(n_scr[...], w_ref[...], preferred_element_type=jnp.float32).astype(o_ref.dtype)


def _norm_proj(x, nw, w, out_dtype):
    n, d = x.shape
    e = w.shape[1]
    tm, tn = min(PROJ_TM, n), min(PROJ_TN, e)
    return pl.pallas_call(
        _norm_proj_kernel,
        grid=(n // tm, e // tn),
        in_specs=[pl.BlockSpec((tm, d), lambda i, j: (i, 0)),
                  pl.BlockSpec((1, d), lambda i, j: (0, 0)),
                  pl.BlockSpec((d, tn), lambda i, j: (0, j))],
        out_specs=pl.BlockSpec((tm, tn), lambda i, j: (i, j)),
        out_shape=jax.ShapeDtypeStruct((n, e), out_dtype),
        scratch_shapes=[pltpu.VMEM((tm, d), jnp.bfloat16)],
        compiler_params=pltpu.CompilerParams(dimension_semantics=("arbitrary", "arbitrary")),
        name="norm_proj",
    )(x, nw.reshape(1, d), w)


def _out_proj_kernel(hg_ref, da_ref, w_ref, x_ref, nw_ref, h_ref, n_ref, nb_ref):
    half = hg_ref.shape[1]
    h = (x_ref[...]
         + jnp.dot(hg_ref[...], w_ref[pl.ds(0, half), :], preferred_element_type=jnp.float32)
         + jnp.dot(da_ref[...], w_ref[pl.ds(half, half), :], preferred_element_type=jnp.float32))
    h_ref[...] = h
    n = h * lax.rsqrt(jnp.mean(h * h, axis=-1, keepdims=True) + RMS_EPS) * nw_ref[...]
    n_ref[...] = n
    nb_ref[...] = n.astype(jnp.bfloat16)


def _out_proj(hg, da, w, x, nw):
    n, d = x.shape
    half = hg.shape[1]
    tm = min(PROJ_TM, n)
    row = lambda cols: pl.BlockSpec((tm, cols), lambda i: (i, 0))
    return pl.pallas_call(
        _out_proj_kernel,
        grid=(n // tm,),
        in_specs=[row(half), row(half), pl.BlockSpec((2 * half, d), lambda i: (0, 0)), row(d),
                  pl.BlockSpec((1, d), lambda i: (0, 0))],
        out_specs=[row(d), row(d), row(d)],
        out_shape=[jax.ShapeDtypeStruct((n, d), jnp.float32), jax.ShapeDtypeStruct((n, d), jnp.float32),
                   jax.ShapeDtypeStruct((n, d), jnp.bfloat16)],
        compiler_params=pltpu.CompilerParams(dimension_semantics=("arbitrary",)),
        name="out_proj",
    )(hg, da, w, x, nw.reshape(1, d))


SUB = 16
HG_SEQ_BLOCK = 512


def _split_dot(tri, a):
    hi = a.astype(jnp.bfloat16)
    lo = (a - hi.astype(jnp.float32)).astype(jnp.bfloat16)
    return (jnp.dot(tri, hi, preferred_element_type=jnp.float32)
            + jnp.dot(tri, lo, preferred_element_type=jnp.float32))


def _hgrn2_kernel(q_ref, f_ref, v_ref, g_ref, lb_ref, nw_ref, o_ref, state_t):
    @pl.when(pl.program_id(2) == 0)
    def _():
        state_t[...] = jnp.zeros_like(state_t)

    lp = lb_ref[...]
    le = jnp.exp(lp - jnp.max(lp, axis=0, keepdims=True))
    lb = le[0:1] / jnp.sum(le, axis=0, keepdims=True)
    nw = nw_ref[...]
    r64 =lax.broadcasted_iota(jnp.int32, (CHUNK, CHUNK), 0)
    c64 = lax.broadcasted_iota(jnp.int32, (CHUNK, CHUNK), 1)
    tri = (c64 <= r64).astype(jnp.bfloat16)
    row_k = lax.broadcasted_iota(jnp.int32, (CHUNK, HG_DK), 0)
    row_s = lax.broadcasted_iota(jnp.int32, (SUB, HG_DK), 0)
    lane_s = lax.broadcasted_iota(jnp.int32, (SUB, CHUNK), 1)

    def chunk(ci, carry):
        r0 = pl.multiple_of(ci * CHUNK, CHUNK)
        f = lb + (1.0 - lb) * jax.nn.sigmoid(f_ref[pl.ds(r0, CHUNK), :])
        k = 1.0 - f
        b = _split_dot(tri, jnp.log(f))
        q = q_ref[pl.ds(r0, CHUNK), :] * (HG_DK ** -0.5)
        v = v_ref[pl.ds(r0, CHUNK), :]
        strips = []
        for i in range(CHUNK // SUB):
            lo = i * SUB
            bi, qi, ki = b[lo:lo + SUB], q[lo:lo + SUB], k[lo:lo + SUB]
            diag = jnp.zeros((SUB, CHUNK), jnp.float32)
            for s in range(SUB):
                dec = jnp.exp(jnp.where(row_s >= s, bi - bi[s:s + 1], -jnp.inf))
                col = jnp.sum(qi * ki[s:s + 1] * dec, axis=-1, keepdims=True)
                diag = jnp.where(lane_s == lo + s, col, diag)
            if i == 0:
                strips.append(diag)
            else:
                ref = b[lo - 1:lo]
                qt = (qi * jnp.exp(bi - ref)).astype(jnp.bfloat16)
                kt = (k * jnp.exp(jnp.where(row_k < lo, ref - b, -jnp.inf))).astype(jnp.bfloat16)
                strips.append(diag + lax.dot_general(qt, kt, (((1,), (1,)), ((), ())),
                                                     preferred_element_type=jnp.float32))
        a = jnp.concatenate(strips, axis=0).astype(jnp.bfloat16)
        vb = v.astype(jnp.bfloat16)
        st = state_t[...]
        o = jnp.dot(a, vb, preferred_element_type=jnp.float32)
        o = o + lax.dot_general((q * jnp.exp(b)).astype(jnp.bfloat16), st.astype(jnp.bfloat16),
                                (((1,), (1,)), ((), ())), preferred_element_type=jnp.float32)
        b_last = b[CHUNK - 1:CHUNK]
        kd = (k * jnp.exp(b_last - b)).astype(jnp.bfloat16)
        state_t[...] = st * jnp.exp(b_last) + jnp.dot(v.T.astype(jnp.bfloat16), kd,
                                                      preferred_element_type=jnp.float32)
        y = o * lax.rsqrt(jnp.mean(o * o, axis=-1, keepdims=True) + RMS_EPS) * nw
        o_ref[pl.ds(r0, CHUNK), :] = (y * jax.nn.silu(g_ref[pl.ds(r0, CHUNK), :])).astype(o_ref.dtype)
        return carry

    lax.fori_loop(0, q_ref.shape[0] // CHUNK, chunk, 0)


def _hgrn2_pallas(proj, lb_param, norm_w, col0=0):
    bsz, seq, _ = proj.shape
    sb = min(HG_SEQ_BLOCK, seq)
    c0 = col0 // HG_DK
    nl = lb_param.shape[0]

    def col(group):
        return pl.BlockSpec((None, sb, HG_DK), lambda b, h, s: (b, s, c0 + group * HG_HEADS + h))

    return pl.pallas_call(
        _hgrn2_kernel,
        grid=(bsz, HG_HEADS, seq // sb),
        in_specs=[col(0), col(1), col(2), col(3),
                  pl.BlockSpec((nl, HG_DK), lambda b, h, s: (0, h)),
                  pl.BlockSpec((1, HG_DV), lambda b, h, s: (0, 0))],
        out_specs=pl.BlockSpec((None, sb, HG_DV), lambda b, h, s: (b, s, h)),
        out_shape=jax.ShapeDtypeStruct((bsz, seq, HG_HEADS * HG_DV), jnp.bfloat16),
        scratch_shapes=[pltpu.VMEM((HG_DV, HG_DK), jnp.float32)],
        compiler_params=pltpu.CompilerParams(dimension_semantics=("arbitrary", "arbitrary", "arbitrary")),
        name="hgrn2",
    )(proj, proj, proj, proj, lb_param, norm_w.reshape(1, HG_DV))


DA_TILE = 256
LAMBDA_INIT = 0.8 - 0.6 * math.exp(-0.3 * 0)


def _diff_attn_kernel(q_ref, k_ref, v_ref, lp_ref, nw_ref, o_ref, m_scr, l_scr, acc_scr):
    t = q_ref.shape[0]
    qi = pl.program_id(2)
    lane = lax.broadcasted_iota(jnp.int32, (t, 2 * DA_DH), 1)
    q = q_ref[...] * jnp.asarray(DA_DH ** -0.5, jnp.bfloat16)
    qs = [jnp.where(lane < DA_DH, q, jnp.zeros_like(q)), jnp.where(lane >= DA_DH, q, jnp.zeros_like(q))]
    m_scr[...] = jnp.full(m_scr.shape, -jnp.inf, jnp.float32)
    l_scr[...] = jnp.zeros_like(l_scr)
    acc_scr[...] = jnp.zeros_like(acc_scr)

    def sweep(kj, mask):
        k0 = pl.multiple_of(kj * t, t)
        kt = k_ref[pl.ds(k0, t), :]
        vt = v_ref[pl.ds(k0, t), :]
        for c in range(2):
            s = lax.dot_general(qs[c], kt, (((1,), (1,)), ((), ())), preferred_element_type=jnp.float32)
            if mask is not None:
                s = jnp.where(mask, s, -jnp.inf)
            m_old = m_scr[c]
            m_new = jnp.maximum(m_old, jnp.max(s, axis=-1, keepdims=True))
            alpha = jnp.exp(m_old - m_new)
            p = jnp.exp(s - m_new)
            l_scr[c] = alpha * l_scr[c] + jnp.sum(p, axis=-1, keepdims=True)
            acc_scr[c] = alpha * acc_scr[c] + jnp.dot(p.astype(jnp.bfloat16), vt,
                                                      preferred_element_type=jnp.float32)
            m_scr[c] = m_new

    def full_tile(kj, carry):
        sweep(kj, None)
        return carry

    lax.fori_loop(0, qi, full_tile, 0)
    row = lax.broadcasted_iota(jnp.int32, (t, t), 0)
    col = lax.broadcasted_iota(jnp.int32, (t, t), 1)
    sweep(qi, (col // CHUNK) <= (row // CHUNK))

    lp = lp_ref[...]
    lam = (jnp.exp(jnp.sum(lp[0:1] * lp[1:2], axis=-1, keepdims=True))
           - jnp.exp(jnp.sum(lp[2:3] * lp[3:4], axis=-1, keepdims=True)) + LAMBDA_INIT)
    o = acc_scr[0] / l_scr[0] - lam * (acc_scr[1] / l_scr[1])
    y = o * lax.rsqrt(jnp.mean(o * o, axis=-1, keepdims=True) + RMS_EPS) * nw_ref[...]
    o_ref[...] = (y * (1.0 - LAMBDA_INIT)).astype(o_ref.dtype)


def _diff_attn_pallas(q, k, v, lam_params, norm_w, cols=(0, 0, 0)):
    bsz, seq, _ = q.shape
    t = min(DA_TILE, seq)
    cq, ck, cv = cols
    return pl.pallas_call(
        _diff_attn_kernel,
        grid=(bsz, DA_HEADS, seq // t),
        in_specs=[pl.BlockSpec((None, t, 2 * DA_DH), lambda b, h, i: (b, i, cq + h)),
                  pl.BlockSpec((None, seq, 2 * DA_DH), lambda b, h, i: (b, 0, ck + h)),
                  pl.BlockSpec((None, seq, DA_DV), lambda b, h, i: (b, 0, cv + h)),
                  pl.BlockSpec((4, DA_DH), lambda b, h, i: (0, 0)),
                  pl.BlockSpec((1, DA_DV), lambda b, h, i: (0, 0))],
        out_specs=pl.BlockSpec((None, t, DA_DV), lambda b, h, i: (b, i, h)),
        out_shape=jax.ShapeDtypeStruct((bsz, seq, DA_HEADS * DA_DV), jnp.bfloat16),
        scratch_shapes=[pltpu.VMEM((2, t, 1), jnp.float32), pltpu.VMEM((2, t, 1), jnp.float32),
                        pltpu.VMEM((2, t, DA_DV), jnp.float32)],
        compiler_params=pltpu.CompilerParams(dimension_semantics=("arbitrary", "arbitrary", "arbitrary")),
        name="diff_attn",
    )(q, k, v, lam_params, norm_w.reshape(1, DA_DV))


PEER_N_EXPERTS = PEER_N_KEYS * PEER_N_KEYS
PEER_SEL = PEER_HEADS * PEER_TOPK
SLAB = D_MODEL // 2 // 128
HALF_D = D_MODEL // 2
_CAND_NB = [PEER_TOPK // (a + 1) for a in range(PEER_TOPK)]
_CAND_ROWS = 56


def _topk_rows(s, k):
    n = s.shape[0]
    iota = lax.broadcasted_iota(jnp.int32, s.shape, 0)
    vals, idxs = [], []
    for _ in range(k):
        m = jnp.max(s, axis=0, keepdims=True)
        idx = jnp.min(jnp.where(s == m, iota, n), axis=0, keepdims=True)
        s = jnp.where(iota == idx, -jnp.inf, s)
        vals.append(m)
        idxs.append(idx)
    return jnp.concatenate(vals, axis=0), jnp.concatenate(idxs, axis=0)


def _peer_topk_kernel(n2_ref, wqt_ref, sk_ref, idx_ref, g_ref, qh_scr):
    qh_scr[...] = lax.dot_general(wqt_ref[...], n2_ref[...], (((1,), (1,)), ((), ())),
                                  preferred_element_type=jnp.float32)

    def head(h, carry):
        base = pl.multiple_of(h * PEER_D_KEY, PEER_D_KEY)
        q1 = qh_scr[pl.ds(base, PEER_HALF), :].astype(jnp.bfloat16)
        q2 = qh_scr[pl.ds(base + PEER_HALF, PEER_HALF), :].astype(jnp.bfloat16)
        s1 = jnp.dot(sk_ref[h, 0], q1, preferred_element_type=jnp.float32)
        s2 = jnp.dot(sk_ref[h, 1], q2, preferred_element_type=jnp.float32)
        v1, i1 = _topk_rows(s1, PEER_TOPK)
        v2, i2 = _topk_rows(s2, PEER_TOPK)
        tb = v1.shape[1]
        cand = jnp.concatenate(
            [v1[a:a + 1] + v2[:_CAND_NB[a]] for a in range(PEER_TOPK)]
            + [jnp.full((_CAND_ROWS - sum(_CAND_NB), tb), -jnp.inf, jnp.float32)], axis=0)
        cidx = jnp.concatenate(
            [i1[a:a + 1] * PEER_N_KEYS + i2[:_CAND_NB[a]] for a in range(PEER_TOPK)]
            + [jnp.zeros((_CAND_ROWS - sum(_CAND_NB), tb), jnp.int32)], axis=0)
        ts, pos = _topk_rows(cand, PEER_TOPK)
        iota = lax.broadcasted_iota(jnp.int32, cand.shape, 0)
        eid = jnp.concatenate(
            [jnp.sum(jnp.where(iota == pos[r:r + 1], cidx, 0), axis=0, keepdims=True)
             for r in range(PEER_TOPK)], axis=0)
        e = jnp.exp(ts - ts[0:1])
        g = e / jnp.sum(e, axis=0, keepdims=True)
        row = pl.multiple_of(h * PEER_TOPK, PEER_TOPK)
        idx_ref[pl.ds(row, PEER_TOPK), :] = eid
        g_ref[pl.ds(row, PEER_TOPK), :] = g
        return carry

    lax.fori_loop(0, PEER_HEADS, head, 0)


def _peer_topk(n2b, wqt, sk, tb=256):
    n = n2b.shape[0]
    return pl.pallas_call(
        _peer_topk_kernel,
        grid=(n // tb,),
        in_specs=[pl.BlockSpec((tb, D_MODEL), lambda i: (i, 0)),
                  pl.BlockSpec(wqt.shape, lambda i: (0, 0)),
                  pl.BlockSpec(sk.shape, lambda i: (0, 0, 0, 0))],
        out_specs=[pl.BlockSpec((PEER_SEL, tb), lambda i: (0, i)),
                   pl.BlockSpec((PEER_SEL, tb), lambda i: (0, i))],
        out_shape=[jax.ShapeDtypeStruct((PEER_SEL, n), jnp.int32),
                   jax.ShapeDtypeStruct((PEER_SEL, n), jnp.float32)],
        scratch_shapes=[pltpu.VMEM((PEER_HEADS * PEER_D_KEY, tb), jnp.float32)],
        compiler_params=pltpu.CompilerParams(dimension_semantics=("arbitrary",),
                                             vmem_limit_bytes=40 * 1024 * 1024),
        name="peer_topk",
    )(n2b, wqt, sk)


def _pack_table(t):
    e = t.shape[0]
    tb = t.astype(jnp.bfloat16).reshape(e, 2, SLAB, 128)
    bits = lax.bitcast_convert_type(tb, jnp.uint16).astype(jnp.uint32)
    word = bits[:, 0] | (bits[:, 1] << 16)
    return lax.bitcast_convert_type(word, jnp.int32).reshape(e * SLAB, 128)


def _gather_rows(idx_ref, base, tab_ref, tile):
    for mi in range(PEER_SEL):
        i = pl.multiple_of(idx_ref[base + mi], SLAB)
        tile[pl.ds(SLAB * mi, SLAB), :] = tab_ref[pl.ds(i, SLAB), :]


def _gathered_matrix(tile):
    return jnp.concatenate(
        [pltpu.bitcast(tile[pl.ds(j, PEER_SEL, stride=SLAB), :], jnp.bfloat16) for j in range(SLAB)],
        axis=-1)


GROUP = 8


def _peer_u_kernel(idx_ref, tab_ref, x_ref, g_ref, w_ref, tile_a, tile_b, r_scr):
    tb = x_ref.shape[0]

    def group(gi, carry):
        r0 = pl.multiple_of(gi * GROUP, GROUP)
        xg = x_ref[pl.ds(r0, GROUP), :]
        rs = []
        for k in range(GROUP):
            tile = tile_a if k % 2 == 0 else tile_b
            _gather_rows(idx_ref, (r0 + k) * PEER_SEL, tab_ref, tile)
            x2 = jnp.concatenate([xg[k:k + 1, :HALF_D], xg[k:k + 1, HALF_D:]], axis=0).astype(jnp.bfloat16)
            rs.append(lax.dot_general(x2, _gathered_matrix(tile), (((1,), (1,)), ((), ())),
                                      preferred_element_type=jnp.float32))
        r_scr[pl.ds(pl.multiple_of(gi * 2 * GROUP, 2 * GROUP), 2 * GROUP), :] = jnp.concatenate(rs, axis=0)
        return carry

    lax.fori_loop(0, tb // GROUP, group, 0)

    e = r_scr[...]
    nl = e.shape[1]
    row = lax.broadcasted_iota(jnp.int32, e.shape, 0)
    lane = lax.broadcasted_iota(jnp.int32, e.shape, 1)
    nxt = pltpu.roll(pltpu.roll(e, e.shape[0] - 1, axis=0), nl - 1, axis=1)
    prv = pltpu.roll(pltpu.roll(e, 1, axis=0), 1, axis=1)
    act = e + jnp.where(row % 2 == 0, nxt, prv)
    w = jnp.where(row % 2 == lane % 2, g_ref[...] * jax.nn.gelu(act), 0.0)
    w_ref[...] = w.astype(jnp.bfloat16)


def _peer_v_kernel(idx_ref, tab_ref, w_ref, h_ref, nw_ref, o_ref, tile_a, tile_b, r_scr):
    tb = h_ref.shape[0]

    def group(gi, carry):
        r0 = pl.multiple_of(gi * GROUP, GROUP)
        wg = w_ref[pl.ds(pl.multiple_of(gi * 2 * GROUP, 2 * GROUP), 2 * GROUP), :]
        rs = []
        for k in range(GROUP):
            tile = tile_a if k % 2 == 0 else tile_b
            _gather_rows(idx_ref, (r0 + k) * PEER_SEL, tab_ref, tile)
            rs.append(jnp.dot(wg[2 * k:2 * k + 2], _gathered_matrix(tile),
                              preferred_element_type=jnp.float32))
        r_scr[pl.ds(r0, GROUP), :HALF_D] = jnp.concatenate([r[0:1] for r in rs], axis=0)
        r_scr[pl.ds(r0, GROUP), HALF_D:] = jnp.concatenate([r[1:2] for r in rs], axis=0)
        return carry

    lax.fori_loop(0, tb // GROUP, group, 0)

    y = h_ref[...] + r_scr[...]
    y = y * lax.rsqrt(jnp.mean(y * y, axis=-1, keepdims=True) + RMS_EPS)
    o_ref[...] = y * nw_ref[...]


_PEER_VMEM = 56 * 1024 * 1024


def _peer_u(idx_flat, tab, x, g2, tb=64):
    n = x.shape[0]
    return pl.pallas_call(
        _peer_u_kernel,
        grid=(n // tb,),
        in_specs=[pl.BlockSpec((tb * PEER_SEL,), lambda i: (i,), memory_space=pltpu.SMEM),
                  pl.BlockSpec(memory_space=pltpu.VMEM),
                  pl.BlockSpec((tb, D_MODEL), lambda i: (i, 0)),
                  pl.BlockSpec((2 * tb, 2 * PEER_SEL), lambda i: (i, 0))],
        out_specs=pl.BlockSpec((2 * tb, 2 * PEER_SEL), lambda i: (i, 0)),
        out_shape=jax.ShapeDtypeStruct((2 * n, 2 * PEER_SEL), jnp.bfloat16),
        scratch_shapes=[pltpu.VMEM((PEER_SEL * SLAB, 128), jnp.int32),
                        pltpu.VMEM((PEER_SEL * SLAB, 128), jnp.int32),
                        pltpu.VMEM((2 * tb, 2 * PEER_SEL), jnp.float32)],
        compiler_params=pltpu.CompilerParams(dimension_semantics=("arbitrary",),
                                             vmem_limit_bytes=_PEER_VMEM),
        name="peer_u",
    )(idx_flat, tab, x, g2)


def _peer_v(idx_flat, tab, w, h, nw, tb=64):
    n = h.shape[0]
    return pl.pallas_call(
        _peer_v_kernel,
        grid=(n // tb,),
        in_specs=[pl.BlockSpec((tb * PEER_SEL,), lambda i: (i,), memory_space=pltpu.SMEM),
                  pl.BlockSpec(memory_space=pltpu.VMEM),
                  pl.BlockSpec((2 * tb, 2 * PEER_SEL), lambda i: (i, 0)),
                  pl.BlockSpec((tb, D_MODEL), lambda i: (i, 0)),
                  pl.BlockSpec((1, D_MODEL), lambda i: (0, 0))],
        out_specs=pl.BlockSpec((tb, D_MODEL), lambda i: (i, 0)),
        out_shape=jax.ShapeDtypeStruct((n, D_MODEL), jnp.float32),
        scratch_shapes=[pltpu.VMEM((PEER_SEL * SLAB, 128), jnp.int32),
                        pltpu.VMEM((PEER_SEL * SLAB, 128), jnp.int32),
                        pltpu.VMEM((tb, D_MODEL), jnp.float32)],
        compiler_params=pltpu.CompilerParams(dimension_semantics=("arbitrary",),
                                             vmem_limit_bytes=_PEER_VMEM),
        name="peer_v",
    )(idx_flat, tab, w, h, nw)


def _peer_block(n2, n2b, h, w_query, sub_keys, u, v, norm_final_w):
    n = n2.shape[0]
    wqt = w_query.T.astype(jnp.bfloat16)
    sk = sub_keys.astype(jnp.bfloat16)
    idx_t, g_t = _peer_topk(n2b, wqt, sk)
    idx_flat = (idx_t.T * SLAB).reshape(n * PEER_SEL)
    g2 = jnp.repeat(jnp.repeat(g_t.T, 2, axis=1), 2, axis=0)
    w = _peer_u(idx_flat, _pack_table(u), n2, g2)
    return _peer_v(idx_flat, _pack_table(v), w, h, norm_final_w.reshape(1, D_MODEL))


HG_COLS = HG_HEADS * (2 * HG_DK + 2 * HG_DV)


def kernel(x, norm_mix_w, w_in, hgrn_lb_param, hgrn_norm_w, diff_lambda, diff_norm_w,
           w_out, norm_ffn_w, peer_w_query, peer_sub_keys, peer_u, peer_v, norm_final_w):
    bsz, seq, d = x.shape
    n = bsz * seq
    x2 = x.reshape(n, d)
    w_in_b = w_in[0].astype(jnp.bfloat16)
    proj_hg = _norm_proj(x2, norm_mix_w[0], w_in_b[:, :HG_COLS], jnp.float32).reshape(bsz, seq, HG_COLS)
    proj_da = _norm_proj(x2, norm_mix_w[0], w_in_b[:, HG_COLS:], jnp.bfloat16).reshape(bsz, seq, -1)
    hg_out = _hgrn2_pallas(proj_hg, hgrn_lb_param, hgrn_norm_w[0])
    da_out = _diff_attn_pallas(proj_da, proj_da, proj_da, diff_lambda[0], diff_norm_w[0],
                               cols=(0, DA_HEADS, 2 * DA_HEADS))
    h, n2, n2b = _out_proj(hg_out.reshape(n, -1), da_out.reshape(n, -1), w_out[0].astype(jnp.bfloat16),
                           x2, norm_ffn_w[0])
    out = _peer_block(n2, n2b, h, peer_w_query[0], peer_sub_keys[0], peer_u[0], peer_v[0], norm_final_w)
    return out.reshape(bsz, seq, d)
```

```python
import math
import jax, jax.numpy as jnp
from jax import lax
from jax.experimental import pallas as pl
from jax.experimental.pallas import tpu as pltpu

D_MODEL = 1024
BATCH = 8
SEQ = 4096
CHUNK = 64
HG_HEADS = 4
HG_DK = 128
HG_DV = 128
DA_HEADS = 4
DA_DV = 128
DA_DH = 64
Q_BLOCK = 128
PEER_HEADS = 8
PEER_N_KEYS = 128
PEER_D_KEY = 256
PEER_HALF = 128
PEER_TOPK = 16
TOKEN_BLOCK = 128
RMS_EPS = 1e-6


def _rmsnorm(x, w):
    y = x * lax.rsqrt(jnp.mean(x * x, axis=-1, keepdims=True) + RMS_EPS)
    return y * w


def _hgrn2(q, f_logit, inp, gate, lb, norm_w):
    B, S = q.shape[0], q.shape[1]
    n_chunks = S // CHUNK
    f = lb + (1.0 - lb) * jax.nn.sigmoid(f_logit)
    k = 1.0 - f
    log_f = jnp.log(f)
    qf = q * (HG_DK ** -0.5)

    def to_chunks(t, d):
        return t.reshape(B, n_chunks, CHUNK, HG_HEADS, d).transpose(1, 0, 3, 2, 4)

    qc, kc, ac = to_chunks(qf, HG_DK), to_chunks(k, HG_DK), to_chunks(log_f, HG_DK)
    vc = to_chunks(inp, HG_DV)
    tri = jnp.tril(jnp.ones((CHUNK, CHUNK), dtype=bool))

    def step(state, xs):
        qb, kb, vb, ab = xs
        b = jnp.cumsum(ab, axis=2)
        diff = b[:, :, :, None, :] - b[:, :, None, :, :]
        decay = jnp.exp(jnp.where(tri[:, :, None], diff, -jnp.inf))
        scores = jnp.einsum('bhtd,bhsd,bhtsd->bhts', qb, kb, decay)
        o = jnp.einsum('bhts,bhse->bhte', scores, vb) \
            + jnp.einsum('bhtd,bhde->bhte', qb * jnp.exp(b), state)
        b_last = b[:, :, -1]
        state = jnp.exp(b_last)[..., None] * state \
            + jnp.einsum('bhsd,bhse->bhde', kb * jnp.exp(b_last[:, :, None, :] - b), vb)
        return state, o

    s0 = jnp.zeros((B, HG_HEADS, HG_DK, HG_DV), jnp.float32)
    _, oc = lax.scan(step, s0, (qc, kc, vc, ac))
    o = oc.transpose(1, 0, 3, 2, 4).reshape(B, S, HG_HEADS, HG_DV)
    g = gate.reshape(B, S, HG_HEADS, HG_DV)
    o = _rmsnorm(o, norm_w) * jax.nn.silu(g)
    return o.reshape(B, S, HG_HEADS * HG_DV)


def _diff_attn(q, k, v, lam_params, norm_w):
    B, S = q.shape[0], q.shape[1]
    n_blocks = S // Q_BLOCK
    lam_init = 0.8 - 0.6 * math.exp(0.0)
    lp = lam_params
    lam = jnp.exp(jnp.sum(lp[0] * lp[1])) - jnp.exp(jnp.sum(lp[2] * lp[3])) + lam_init
    scale = DA_DH ** -0.5
    qf = q.reshape(B, n_blocks, Q_BLOCK, DA_HEADS, 2, DA_DH).transpose(1, 0, 2, 3, 4, 5)
    kf = k.reshape(B, S, DA_HEADS, 2, DA_DH)
    vf = v.reshape(B, S, DA_HEADS, DA_DV)
    key_chunk = jnp.arange(S) // CHUNK

    def one_block(args):
        qb, blk = args
        s = jnp.einsum('bqhcd,bkhcd->bhcqk', qb, kf) * scale
        q_chunk = (blk * Q_BLOCK + jnp.arange(Q_BLOCK)) // CHUNK
        mask = key_chunk[None, :] <= q_chunk[:, None]
        p = jax.nn.softmax(jnp.where(mask, s, -jnp.inf), axis=-1)
        a = p[:, :, 0] - lam * p[:, :, 1]
        return jnp.einsum('bhqk,bkhe->bqhe', a, vf)

    o = lax.map(one_block, (qf, jnp.arange(n_blocks)))
    o = o.transpose(1, 0, 2, 3, 4).reshape(B, S, DA_HEADS, DA_DV)
    o = _rmsnorm(o, norm_w) * (1.0 - lam_init)
    return o.reshape(B, S, DA_HEADS * DA_DV)


PROJ_TM = 512
PROJ_TN = 512


def _norm_proj_kernel(x_ref, nw_ref, w_ref, o_ref, n_scr):
    @pl.when(pl.program_id(1) == 0)
    def _():
        x = x_ref[...]
        y = x * lax.rsqrt(jnp.mean(x * x, axis=-1, keepdims=True) + RMS_EPS) * nw_ref[...]
        n_scr[...] = y.astype(jnp.bfloat16)

    o_ref[...] = jnp.dot(n_scr[...], w_ref[...], preferred_element_type=jnp.float32).astype(o_ref.dtype)


def _norm_proj(x, nw, w, out_dtype):
    n, d = x.shape
    e = w.shape[1]
    tm, tn = min(PROJ_TM, n), min(PROJ_TN, e)
    return pl.pallas_call(
        _norm_proj_kernel,
        grid=(n // tm, e // tn),
        in_specs=[pl.BlockSpec((tm, d), lambda i, j: (i, 0)),
                  pl.BlockSpec((1, d), lambda i, j: (0, 0)),
                  pl.BlockSpec((d, tn), lambda i, j: (0, j))],
        out_specs=pl.BlockSpec((tm, tn), lambda i, j: (i, j)),
        out_shape=jax.ShapeDtypeStruct((n, e), out_dtype),
        scratch_shapes=[pltpu.VMEM((tm, d), jnp.bfloat16)],
        compiler_params=pltpu.CompilerParams(dimension_semantics=("arbitrary", "arbitrary")),
        name="norm_proj",
    )(x, nw.reshape(1, d), w)


def _out_proj_kernel(hg_ref, da_ref, w_ref, x_ref, nw_ref, h_ref, n_ref, nb_ref):
    half = hg_ref.shape[1]
    h = (x_ref[...]
         + jnp.dot(hg_ref[...], w_ref[pl.ds(0, half), :], preferred_element_type=jnp.float32)
         + jnp.dot(da_ref[...], w_ref[pl.ds(half, half), :], preferred_element_type=jnp.float32))
    h_ref[...] = h
    n = h * lax.rsqrt(jnp.mean(h * h, axis=-1, keepdims=True) + RMS_EPS) * nw_ref[...]
    n_ref[...] = n
    nb_ref[...] = n.astype(jnp.bfloat16)


def _out_proj(hg, da, w, x, nw):
    n, d = x.shape
    half = hg.shape[1]
    tm = min(PROJ_TM, n)
    row = lambda cols: pl.BlockSpec((tm, cols), lambda i: (i, 0))
    return pl.pallas_call(
        _out_proj_kernel,
        grid=(n // tm,),
        in_specs=[row(half), row(half), pl.BlockSpec((2 * half, d), lambda i: (0, 0)), row(d),
                  pl.BlockSpec((1, d), lambda i: (0, 0))],
        out_specs=[row(d), row(d), row(d)],
        out_shape=[jax.ShapeDtypeStruct((n, d), jnp.float32), jax.ShapeDtypeStruct((n, d), jnp.float32),
                   jax.ShapeDtypeStruct((n, d), jnp.bfloat16)],
        compiler_params=pltpu.CompilerParams(dimension_semantics=("arbitrary",)),
        name="out_proj",
    )(hg, da, w, x, nw.reshape(1, d))


SUB = 16
HG_SEQ_BLOCK = 512


def _split_dot(tri, a):
    hi = a.astype(jnp.bfloat16)
    lo = (a - hi.astype(jnp.float32)).astype(jnp.bfloat16)
    return (jnp.dot(tri, hi, preferred_element_type=jnp.float32)
            + jnp.dot(tri, lo, preferred_element_type=jnp.float32))


def _hgrn2_kernel(q_ref, f_ref, v_ref, g_ref, lb_ref, nw_ref, o_ref, state_t):
    @pl.when(pl.program_id(2) == 0)
    def _():
        state_t[...] = jnp.zeros_like(state_t)

    lp = lb_ref[...]
    le = jnp.exp(lp - jnp.max(lp, axis=0, keepdims=True))
    lb = le[0:1] / jnp.sum(le, axis=0, keepdims=True)
    nw = nw_ref[...]
    r64 =lax.broadcasted_iota(jnp.int32, (CHUNK, CHUNK), 0)
    c64 = lax.broadcasted_iota(jnp.int32, (CHUNK, CHUNK), 1)
    tri = (c64 <= r64).astype(jnp.bfloat16)
    row_k = lax.broadcasted_iota(jnp.int32, (CHUNK, HG_DK), 0)
    row_s = lax.broadcasted_iota(jnp.int32, (SUB, HG_DK), 0)
    lane_s = lax.broadcasted_iota(jnp.int32, (SUB, CHUNK), 1)

    def chunk(ci, carry):
        r0 = pl.multiple_of(ci * CHUNK, CHUNK)
        f = lb + (1.0 - lb) * jax.nn.sigmoid(f_ref[pl.ds(r0, CHUNK), :])
        k = 1.0 - f
        b = _split_dot(tri, jnp.log(f))
        q = q_ref[pl.ds(r0, CHUNK), :] * (HG_DK ** -0.5)
        v = v_ref[pl.ds(r0, CHUNK), :]
        strips = []
        for i in range(CHUNK // SUB):
            lo = i * SUB
            bi, qi, ki = b[lo:lo + SUB], q[lo:lo + SUB], k[lo:lo + SUB]
            diag = jnp.zeros((SUB, CHUNK), jnp.float32)
            for s in range(SUB):
                dec = jnp.exp(jnp.where(row_s >= s, bi - bi[s:s + 1], -jnp.inf))
                col = jnp.sum(qi * ki[s:s + 1] * dec, axis=-1, keepdims=True)
                diag = jnp.where(lane_s == lo + s, col, diag)
            if i == 0:
                strips.append(diag)
            else:
                ref = b[lo - 1:lo]
                qt = (qi * jnp.exp(bi - ref)).astype(jnp.bfloat16)
                kt = (k * jnp.exp(jnp.where(row_k < lo, ref - b, -jnp.inf))).astype(jnp.bfloat16)
                strips.append(diag + lax.dot_general(qt, kt, (((1,), (1,)), ((), ())),
                                                     preferred_element_type=jnp.float32))
        a = jnp.concatenate(strips, axis=0).astype(jnp.bfloat16)
        vb = v.astype(jnp.bfloat16)
        st = state_t[...]
        o = jnp.dot(a, vb, preferred_element_type=jnp.float32)
        o = o + lax.dot_general((q * jnp.exp(b)).astype(jnp.bfloat16), st.astype(jnp.bfloat16),
                                (((1,), (1,)), ((), ())), preferred_element_type=jnp.float32)
        b_last = b[CHUNK - 1:CHUNK]
        kd = (k * jnp.exp(b_last - b)).astype(jnp.bfloat16)
        state_t[...] = st * jnp.exp(b_last) + jnp.dot(v.T.astype(jnp.bfloat16), kd,
                                                      preferred_element_type=jnp.float32)
        y = o * lax.rsqrt(jnp.mean(o * o, axis=-1, keepdims=True) + RMS_EPS) * nw
        o_ref[pl.ds(r0, CHUNK), :] = (y * jax.nn.silu(g_ref[pl.ds(r0, CHUNK), :])).astype(o_ref.dtype)
        return carry

    lax.fori_loop(0, q_ref.shape[0] // CHUNK, chunk, 0)


def _hgrn2_pallas(proj, lb_param, norm_w, col0=0):
    bsz, seq, _ = proj.shape
    sb = min(HG_SEQ_BLOCK, seq)
    c0 = col0 // HG_DK
    nl = lb_param.shape[0]

    def col(group):
        return pl.BlockSpec((None, sb, HG_DK), lambda b, h, s: (b, s, c0 + group * HG_HEADS + h))

    return pl.pallas_call(
        _hgrn2_kernel,
        grid=(bsz, HG_HEADS, seq // sb),
        in_specs=[col(0), col(1), col(2), col(3),
                  pl.BlockSpec((nl, HG_DK), lambda b, h, s: (0, h)),
                  pl.BlockSpec((1, HG_DV), lambda b, h, s: (0, 0))],
        out_specs=pl.BlockSpec((None, sb, HG_DV), lambda b, h, s: (b, s, h)),
        out_shape=jax.ShapeDtypeStruct((bsz, seq, HG_HEADS * HG_DV), jnp.bfloat16),
        scratch_shapes=[pltpu.VMEM((HG_DV, HG_DK), jnp.float32)],
        compiler_params=pltpu.CompilerParams(dimension_semantics=("arbitrary", "arbitrary", "arbitrary")),
        name="hgrn2",
    )(proj, proj, proj, proj, lb_param, norm_w.reshape(1, HG_DV))


DA_TILE = 512
LAMBDA_INIT = 0.8 - 0.6 * math.exp(-0.3 * 0)


def _diff_attn_kernel(q_ref, k_ref, v_ref, lp_ref, nw_ref, o_ref, m_scr, l_scr, acc_scr):
    t = q_ref.shape[0]
    qi = pl.program_id(2)
    lane = lax.broadcasted_iota(jnp.int32, (t, 2 * DA_DH), 1)
    q = q_ref[...] * jnp.asarray(DA_DH ** -0.5, jnp.bfloat16)
    qs = [jnp.where(lane < DA_DH, q, jnp.zeros_like(q)), jnp.where(lane >= DA_DH, q, jnp.zeros_like(q))]
    m_scr[...] = jnp.full(m_scr.shape, -jnp.inf, jnp.float32)
    l_scr[...] = jnp.zeros_like(l_scr)
    acc_scr[...] = jnp.zeros_like(acc_scr)

    def sweep(kj, mask):
        k0 = pl.multiple_of(kj * t, t)
        kt = k_ref[pl.ds(k0, t), :]
        vt = v_ref[pl.ds(k0, t), :]
        for c in range(2):
            s = lax.dot_general(qs[c], kt, (((1,), (1,)), ((), ())), preferred_element_type=jnp.float32)
            if mask is not None:
                s = jnp.where(mask, s, -jnp.inf)
            m_old = m_scr[c]
            m_new = jnp.maximum(m_old, jnp.max(s, axis=-1, keepdims=True))
            alpha = jnp.exp(m_old - m_new)
            p = jnp.exp(s - jnp.tile(m_new, (1, t // 128)))
            l_scr[c] = alpha * l_scr[c] + jnp.sum(p, axis=-1, keepdims=True)
            acc_scr[c] = alpha * acc_scr[c] + jnp.dot(p.astype(jnp.bfloat16), vt,
                                                      preferred_element_type=jnp.float32)
            m_scr[c] = m_new

    def full_tile(kj, carry):
        sweep(kj, None)
        return carry

    lax.fori_loop(0, qi, full_tile, 0)
    row = lax.broadcasted_iota(jnp.int32, (t, t), 0)
    col = lax.broadcasted_iota(jnp.int32, (t, t), 1)
    sweep(qi, (col // CHUNK) <= (row // CHUNK))

    lp = lp_ref[...]
    lam = (jnp.exp(jnp.sum(lp[0:1] * lp[1:2], axis=-1, keepdims=True))
           - jnp.exp(jnp.sum(lp[2:3] * lp[3:4], axis=-1, keepdims=True)) + LAMBDA_INIT)
    o = acc_scr[0] / l_scr[0] - lam * (acc_scr[1] / l_scr[1])
    y = o * lax.rsqrt(jnp.mean(o * o, axis=-1, keepdims=True) + RMS_EPS) * nw_ref[...]
    o_ref[...] = (y * (1.0 - LAMBDA_INIT)).astype(o_ref.dtype)


def _diff_attn_pallas(q, k, v, lam_params, norm_w, cols=(0, 0, 0)):
    bsz, seq, _ = q.shape
    t = min(DA_TILE, seq)
    cq, ck, cv = cols
    return pl.pallas_call(
        _diff_attn_kernel,
        grid=(bsz, DA_HEADS, seq // t),
        in_specs=[pl.BlockSpec((None, t, 2 * DA_DH), lambda b, h, i: (b, i, cq + h)),
                  pl.BlockSpec((None, seq, 2 * DA_DH), lambda b, h, i: (b, 0, ck + h)),
                  pl.BlockSpec((None, seq, DA_DV), lambda b, h, i: (b, 0, cv + h)),
                  pl.BlockSpec((4, DA_DH), lambda b, h, i: (0, 0)),
                  pl.BlockSpec((1, DA_DV), lambda b, h, i: (0, 0))],
        out_specs=pl.BlockSpec((None, t, DA_DV), lambda b, h, i: (b, i, h)),
        out_shape=jax.ShapeDtypeStruct((bsz, seq, DA_HEADS * DA_DV), jnp.bfloat16),
        scratch_shapes=[pltpu.VMEM((2, t, 128), jnp.float32), pltpu.VMEM((2, t, 128), jnp.float32),
                        pltpu.VMEM((2, t, DA_DV), jnp.float32)],
        compiler_params=pltpu.CompilerParams(dimension_semantics=("arbitrary", "arbitrary", "arbitrary")),
        name="diff_attn",
    )(q, k, v, lam_params, norm_w.reshape(1, DA_DV))


PEER_N_EXPERTS = PEER_N_KEYS * PEER_N_KEYS
PEER_SEL = PEER_HEADS * PEER_TOPK
SLAB = D_MODEL // 2 // 128
HALF_D = D_MODEL // 2
_CAND_NB = [PEER_TOPK // (a + 1) for a in range(PEER_TOPK)]
_CAND_ROWS = 56


def _topk_rows(s, k):
    n = s.shape[0]
    iota = lax.broadcasted_iota(jnp.int32, s.shape, 0)
    vals, idxs = [], []
    for _ in range(k):
        m = jnp.max(s, axis=0, keepdims=True)
        idx = jnp.min(jnp.where(s == m, iota, n), axis=0, keepdims=True)
        s = jnp.where(iota == idx, -jnp.inf, s)
        vals.append(m)
        idxs.append(idx)
    return jnp.concatenate(vals, axis=0), jnp.concatenate(idxs, axis=0)


def _peer_topk_kernel(n2_ref, wqt_ref, sk_ref, idx_ref, g_ref, qh_scr):
    qh_scr[...] = lax.dot_general(wqt_ref[...], n2_ref[...], (((1,), (1,)), ((), ())),
                                  preferred_element_type=jnp.float32)

    def head(h, carry):
        base = pl.multiple_of(h * PEER_D_KEY, PEER_D_KEY)
        q1 = qh_scr[pl.ds(base, PEER_HALF), :].astype(jnp.bfloat16)
        q2 = qh_scr[pl.ds(base + PEER_HALF, PEER_HALF), :].astype(jnp.bfloat16)
        s1 = jnp.dot(sk_ref[h, 0], q1, preferred_element_type=jnp.float32)
        s2 = jnp.dot(sk_ref[h, 1], q2, preferred_element_type=jnp.float32)
        v1, i1 = _topk_rows(s1, PEER_TOPK)
        v2, i2 = _topk_rows(s2, PEER_TOPK)
        tb = v1.shape[1]
        cand = jnp.concatenate(
            [v1[a:a + 1] + v2[:_CAND_NB[a]] for a in range(PEER_TOPK)]
            + [jnp.full((_CAND_ROWS - sum(_CAND_NB), tb), -jnp.inf, jnp.float32)], axis=0)
        cidx = jnp.concatenate(
            [i1[a:a + 1] * PEER_N_KEYS + i2[:_CAND_NB[a]] for a in range(PEER_TOPK)]
            + [jnp.zeros((_CAND_ROWS - sum(_CAND_NB), tb), jnp.int32)], axis=0)
        ts, pos = _topk_rows(cand, PEER_TOPK)
        iota = lax.broadcasted_iota(jnp.int32, cand.shape, 0)
        eid = jnp.concatenate(
            [jnp.sum(jnp.where(iota == pos[r:r + 1], cidx, 0), axis=0, keepdims=True)
             for r in range(PEER_TOPK)], axis=0)
        e = jnp.exp(ts - ts[0:1])
        g = e / jnp.sum(e, axis=0, keepdims=True)
        row = pl.multiple_of(h * PEER_TOPK, PEER_TOPK)
        idx_ref[pl.ds(row, PEER_TOPK), :] = eid
        g_ref[pl.ds(row, PEER_TOPK), :] = g
        return carry

    lax.fori_loop(0, PEER_HEADS, head, 0)


def _peer_topk(n2b, wqt, sk, tb=256):
    n = n2b.shape[0]
    return pl.pallas_call(
        _peer_topk_kernel,
        grid=(n // tb,),
        in_specs=[pl.BlockSpec((tb, D_MODEL), lambda i: (i, 0)),
                  pl.BlockSpec(wqt.shape, lambda i: (0, 0)),
                  pl.BlockSpec(sk.shape, lambda i: (0, 0, 0, 0))],
        out_specs=[pl.BlockSpec((PEER_SEL, tb), lambda i: (0, i)),
                   pl.BlockSpec((PEER_SEL, tb), lambda i: (0, i))],
        out_shape=[jax.ShapeDtypeStruct((PEER_SEL, n), jnp.int32),
                   jax.ShapeDtypeStruct((PEER_SEL, n), jnp.float32)],
        scratch_shapes=[pltpu.VMEM((PEER_HEADS * PEER_D_KEY, tb), jnp.float32)],
        compiler_params=pltpu.CompilerParams(dimension_semantics=("arbitrary",),
                                             vmem_limit_bytes=40 * 1024 * 1024),
        name="peer_topk",
    )(n2b, wqt, sk)


def _pack_table(t):
    e = t.shape[0]
    tb = t.astype(jnp.bfloat16).reshape(e, 2, SLAB, 128)
    bits = lax.bitcast_convert_type(tb, jnp.uint16).astype(jnp.uint32)
    word = bits[:, 0] | (bits[:, 1] << 16)
    return lax.bitcast_convert_type(word, jnp.int32).reshape(e * SLAB, 128)


def _gather_rows(idx_ref, base, tab_ref, tile):
    for mi in range(PEER_SEL):
        i = pl.multiple_of(idx_ref[base + mi], SLAB)
        tile[pl.ds(SLAB * mi, SLAB), :] = tab_ref[pl.ds(i, SLAB), :]


def _gathered_matrix(tile):
    return jnp.concatenate(
        [pltpu.bitcast(tile[pl.ds(j, PEER_SEL, stride=SLAB), :], jnp.bfloat16) for j in range(SLAB)],
        axis=-1)


GROUP = 8


def _peer_u_kernel(idx_ref, tab_ref, x_ref, g_ref, w_ref, tile_a, tile_b, r_scr):
    tb = x_ref.shape[0]

    def group(gi, carry):
        r0 = gi * GROUP
        xg = x_ref[pl.ds(r0, GROUP), :]
        rs = []
        for k in range(GROUP):
            tile = tile_a if k % 2 == 0 else tile_b
            _gather_rows(idx_ref, (r0 + k) * PEER_SEL, tab_ref, tile)
            x2 = jnp.concatenate([xg[k:k + 1, :HALF_D], xg[k:k + 1, HALF_D:]], axis=0).astype(jnp.bfloat16)
            rs.append(lax.dot_general(x2, _gathered_matrix(tile), (((1,), (1,)), ((), ())),
                                      preferred_element_type=jnp.float32))
        r_scr[pl.ds(gi * 2 * GROUP, 2 * GROUP), :] = jnp.concatenate(rs, axis=0)
        return carry

    for gi in range(tb // GROUP):
        group(gi, 0)

    e = r_scr[...]
    nl = e.shape[1]
    row = lax.broadcasted_iota(jnp.int32, e.shape, 0)
    lane = lax.broadcasted_iota(jnp.int32, e.shape, 1)
    nxt = pltpu.roll(pltpu.roll(e, e.shape[0] - 1, axis=0), nl - 1, axis=1)
    prv = pltpu.roll(pltpu.roll(e, 1, axis=0), 1, axis=1)
    act = e + jnp.where(row % 2 == 0, nxt, prv)
    w = jnp.where(row % 2 == lane % 2, g_ref[...] * jax.nn.gelu(act), 0.0)
    w_ref[...] = w.astype(jnp.bfloat16)


def _peer_v_kernel(idx_ref, tab_ref, w_ref, h_ref, nw_ref, o_ref, tile_a, tile_b, r_scr):
    tb = h_ref.shape[0]

    def group(gi, carry):
        r0 = gi * GROUP
        wg = w_ref[pl.ds(gi * 2 * GROUP, 2 * GROUP), :]
        rs = []
        for k in range(GROUP):
            tile = tile_a if k % 2 == 0 else tile_b
            _gather_rows(idx_ref, (r0 + k) * PEER_SEL, tab_ref, tile)
            rs.append(jnp.dot(wg[2 * k:2 * k + 2], _gathered_matrix(tile),
                              preferred_element_type=jnp.float32))
        r_scr[pl.ds(r0, GROUP), :HALF_D] = jnp.concatenate([r[0:1] for r in rs], axis=0)
        r_scr[pl.ds(r0, GROUP), HALF_D:] = jnp.concatenate([r[1:2] for r in rs], axis=0)
        return carry

    for gi in range(tb // GROUP):
        group(gi, 0)

    y = h_ref[...] + r_scr[...]
    y = y * lax.rsqrt(jnp.mean(y * y, axis=-1, keepdims=True) + RMS_EPS)
    o_ref[...] = y * nw_ref[...]


_PEER_VMEM = 56 * 1024 * 1024


def _peer_u(idx_flat, tab, x, g2, tb=64):
    n = x.shape[0]
    return pl.pallas_call(
        _peer_u_kernel,
        grid=(n // tb,),
        in_specs=[pl.BlockSpec((tb * PEER_SEL,), lambda i: (i,), memory_space=pltpu.SMEM),
                  pl.BlockSpec(memory_space=pltpu.VMEM),
                  pl.BlockSpec((tb, D_MODEL), lambda i: (i, 0)),
                  pl.BlockSpec((2 * tb, 2 * PEER_SEL), lambda i: (i, 0))],
        out_specs=pl.BlockSpec((2 * tb, 2 * PEER_SEL), lambda i: (i, 0)),
        out_shape=jax.ShapeDtypeStruct((2 * n, 2 * PEER_SEL), jnp.bfloat16),
        scratch_shapes=[pltpu.VMEM((PEER_SEL * SLAB, 128), jnp.int32),
                        pltpu.VMEM((PEER_SEL * SLAB, 128), jnp.int32),
                        pltpu.VMEM((2 * tb, 2 * PEER_SEL), jnp.float32)],
        compiler_params=pltpu.CompilerParams(dimension_semantics=("arbitrary",),
                                             vmem_limit_bytes=_PEER_VMEM),
        name="peer_u",
    )(idx_flat, tab, x, g2)


def _peer_v(idx_flat, tab, w, h, nw, tb=64):
    n = h.shape[0]
    return pl.pallas_call(
        _peer_v_kernel,
        grid=(n // tb,),
        in_specs=[pl.BlockSpec((tb * PEER_SEL,), lambda i: (i,), memory_space=pltpu.SMEM),
                  pl.BlockSpec(memory_space=pltpu.VMEM),
                  pl.BlockSpec((2 * tb, 2 * PEER_SEL), lambda i: (i, 0)),
                  pl.BlockSpec((tb, D_MODEL), lambda i: (i, 0)),
                  pl.BlockSpec((1, D_MODEL), lambda i: (0, 0))],
        out_specs=pl.BlockSpec((tb, D_MODEL), lambda i: (i, 0)),
        out_shape=jax.ShapeDtypeStruct((n, D_MODEL), jnp.float32),
        scratch_shapes=[pltpu.VMEM((PEER_SEL * SLAB, 128), jnp.int32),
                        pltpu.VMEM((PEER_SEL * SLAB, 128), jnp.int32),
                        pltpu.VMEM((tb, D_MODEL), jnp.float32)],
        compiler_params=pltpu.CompilerParams(dimension_semantics=("arbitrary",),
                                             vmem_limit_bytes=_PEER_VMEM),
        name="peer_v",
    )(idx_flat, tab, w, h, nw)


def _peer_block(n2, n2b, h, w_query, sub_keys, u, v, norm_final_w):
    n = n2.shape[0]
    wqt = w_query.T.astype(jnp.bfloat16)
    sk = sub_keys.astype(jnp.bfloat16)
    idx_t, g_t = _peer_topk(n2b, wqt, sk)
    idx_flat = (idx_t.T * SLAB).reshape(n * PEER_SEL)
    g2 = jnp.repeat(jnp.repeat(g_t.T, 2, axis=1), 2, axis=0)
    w = _peer_u(idx_flat, _pack_table(u), n2, g2)
    return _peer_v(idx_flat, _pack_table(v), w, h, norm_final_w.reshape(1, D_MODEL))


HG_COLS = HG_HEADS * (2 * HG_DK + 2 * HG_DV)


def kernel(x, norm_mix_w, w_in, hgrn_lb_param, hgrn_norm_w, diff_lambda, diff_norm_w,
           w_out, norm_ffn_w, peer_w_query, peer_sub_keys, peer_u, peer_v, norm_final_w):
    bsz, seq, d = x.shape
    n = bsz * seq
    x2 = x.reshape(n, d)
    w_in_b = w_in[0].astype(jnp.bfloat16)
    proj_hg = _norm_proj(x2, norm_mix_w[0], w_in_b[:, :HG_COLS], jnp.float32).reshape(bsz, seq, HG_COLS)
    proj_da = _norm_proj(x2, norm_mix_w[0], w_in_b[:, HG_COLS:], jnp.bfloat16).reshape(bsz, seq, -1)
    hg_out = _hgrn2_pallas(proj_hg, hgrn_lb_param, hgrn_norm_w[0])
    da_out = _diff_attn_pallas(proj_da, proj_da, proj_da, diff_lambda[0], diff_norm_w[0],
                               cols=(0, DA_HEADS, 2 * DA_HEADS))
    h, n2, n2b = _out_proj(hg_out.reshape(n, -1), da_out.reshape(n, -1), w_out[0].astype(jnp.bfloat16),
                           x2, norm_ffn_w[0])
    out = _peer_block(n2, n2b, h, peer_w_query[0], peer_sub_keys[0], peer_u[0], peer_v[0], norm_final_w)
    return out.reshape(bsz, seq, d)
```

```python
import math
import jax, jax.numpy as jnp
from jax import lax
from jax.experimental import pallas as pl
from jax.experimental.pallas import tpu as pltpu

D_MODEL = 1024
BATCH = 8
SEQ = 4096
CHUNK = 64
HG_HEADS = 4
HG_DK = 128
HG_DV = 128
DA_HEADS = 4
DA_DV = 128
DA_DH = 64
Q_BLOCK = 128
PEER_HEADS = 8
PEER_N_KEYS = 128
PEER_D_KEY = 256
PEER_HALF = 128
PEER_TOPK = 16
TOKEN_BLOCK = 128
RMS_EPS = 1e-6


def _rmsnorm(x, w):
    y = x * lax.rsqrt(jnp.mean(x * x, axis=-1, keepdims=True) + RMS_EPS)
    return y * w


def _hgrn2(q, f_logit, inp, gate, lb, norm_w):
    B, S = q.shape[0], q.shape[1]
    n_chunks = S // CHUNK
    f = lb + (1.0 - lb) * jax.nn.sigmoid(f_logit)
    k = 1.0 - f
    log_f = jnp.log(f)
    qf = q * (HG_DK ** -0.5)

    def to_chunks(t, d):
        return t.reshape(B, n_chunks, CHUNK, HG_HEADS, d).transpose(1, 0, 3, 2, 4)

    qc, kc, ac = to_chunks(qf, HG_DK), to_chunks(k, HG_DK), to_chunks(log_f, HG_DK)
    vc = to_chunks(inp, HG_DV)
    tri = jnp.tril(jnp.ones((CHUNK, CHUNK), dtype=bool))

    def step(state, xs):
        qb, kb, vb, ab = xs
        b = jnp.cumsum(ab, axis=2)
        diff = b[:, :, :, None, :] - b[:, :, None, :, :]
        decay = jnp.exp(jnp.where(tri[:, :, None], diff, -jnp.inf))
        scores = jnp.einsum('bhtd,bhsd,bhtsd->bhts', qb, kb, decay)
        o = jnp.einsum('bhts,bhse->bhte', scores, vb) \
            + jnp.einsum('bhtd,bhde->bhte', qb * jnp.exp(b), state)
        b_last = b[:, :, -1]
        state = jnp.exp(b_last)[..., None] * state \
            + jnp.einsum('bhsd,bhse->bhde', kb * jnp.exp(b_last[:, :, None, :] - b), vb)
        return state, o

    s0 = jnp.zeros((B, HG_HEADS, HG_DK, HG_DV), jnp.float32)
    _, oc = lax.scan(step, s0, (qc, kc, vc, ac))
    o = oc.transpose(1, 0, 3, 2, 4).reshape(B, S, HG_HEADS, HG_DV)
    g = gate.reshape(B, S, HG_HEADS, HG_DV)
    o = _rmsnorm(o, norm_w) * jax.nn.silu(g)
    return o.reshape(B, S, HG_HEADS * HG_DV)


def _diff_attn(q, k, v, lam_params, norm_w):
    B, S = q.shape[0], q.shape[1]
    n_blocks = S // Q_BLOCK
    lam_init = 0.8 - 0.6 * math.exp(0.0)
    lp = lam_params
    lam = jnp.exp(jnp.sum(lp[0] * lp[1])) - jnp.exp(jnp.sum(lp[2] * lp[3])) + lam_init
    scale = DA_DH ** -0.5
    qf = q.reshape(B, n_blocks, Q_BLOCK, DA_HEADS, 2, DA_DH).transpose(1, 0, 2, 3, 4, 5)
    kf = k.reshape(B, S, DA_HEADS, 2, DA_DH)
    vf = v.reshape(B, S, DA_HEADS, DA_DV)
    key_chunk = jnp.arange(S) // CHUNK

    def one_block(args):
        qb, blk = args
        s = jnp.einsum('bqhcd,bkhcd->bhcqk', qb, kf) * scale
        q_chunk = (blk * Q_BLOCK + jnp.arange(Q_BLOCK)) // CHUNK
        mask = key_chunk[None, :] <= q_chunk[:, None]
        p = jax.nn.softmax(jnp.where(mask, s, -jnp.inf), axis=-1)
        a = p[:, :, 0] - lam * p[:, :, 1]
        return jnp.einsum('bhqk,bkhe->bqhe', a, vf)

    o = lax.map(one_block, (qf, jnp.arange(n_blocks)))
    o = o.transpose(1, 0, 2, 3, 4).reshape(B, S, DA_HEADS, DA_DV)
    o = _rmsnorm(o, norm_w) * (1.0 - lam_init)
    return o.reshape(B, S, DA_HEADS * DA_DV)


PROJ_TM = 512
PROJ_TN = 512


def _norm_proj_kernel(x_ref, nw_ref, w_ref, o_ref, n_scr):
    @pl.when(pl.program_id(1) == 0)
    def _():
        x = x_ref[...]
        y = x * lax.rsqrt(jnp.mean(x * x, axis=-1, keepdims=True) + RMS_EPS) * nw_ref[...]
        n_scr[...] = y.astype(jnp.bfloat16)

    o_ref[...] = jnp.dot(n_scr[...], w_ref[...], preferred_element_type=jnp.float32).astype(o_ref.dtype)


def _norm_proj(x, nw, w, out_dtype):
    n, d = x.shape
    e = w.shape[1]
    tm, tn = min(PROJ_TM, n), min(PROJ_TN, e)
    return pl.pallas_call(
        _norm_proj_kernel,
        grid=(n // tm, e // tn),
        in_specs=[pl.BlockSpec((tm, d), lambda i, j: (i, 0)),
                  pl.BlockSpec((1, d), lambda i, j: (0, 0)),
                  pl.BlockSpec((d, tn), lambda i, j: (0, j))],
        out_specs=pl.BlockSpec((tm, tn), lambda i, j: (i, j)),
        out_shape=jax.ShapeDtypeStruct((n, e), out_dtype),
        scratch_shapes=[pltpu.VMEM((tm, d), jnp.bfloat16)],
        compiler_params=pltpu.CompilerParams(dimension_semantics=("arbitrary", "arbitrary")),
        name="norm_proj",
    )(x, nw.reshape(1, d), w)


def _out_proj_kernel(hg_ref, da_ref, w_ref, x_ref, nw_ref, h_ref, n_ref, nb_ref):
    half = hg_ref.shape[1]
    h = (x_ref[...]
         + jnp.dot(hg_ref[...], w_ref[pl.ds(0, half), :], preferred_element_type=jnp.float32)
         + jnp.dot(da_ref[...], w_ref[pl.ds(half, half), :], preferred_element_type=jnp.float32))
    h_ref[...] = h
    n = h * lax.rsqrt(jnp.mean(h * h, axis=-1, keepdims=True) + RMS_EPS) * nw_ref[...]
    n_ref[...] = n
    nb_ref[...] = n.astype(jnp.bfloat16)


def _out_proj(hg, da, w, x, nw):
    n, d = x.shape
    half = hg.shape[1]
    tm = min(PROJ_TM, n)
    row = lambda cols: pl.BlockSpec((tm, cols), lambda i: (i, 0))
    return pl.pallas_call(
        _out_proj_kernel,
        grid=(n // tm,),
        in_specs=[row(half), row(half), pl.BlockSpec((2 * half, d), lambda i: (0, 0)), row(d),
                  pl.BlockSpec((1, d), lambda i: (0, 0))],
        out_specs=[row(d), row(d), row(d)],
        out_shape=[jax.ShapeDtypeStruct((n, d), jnp.float32), jax.ShapeDtypeStruct((n, d), jnp.float32),
                   jax.ShapeDtypeStruct((n, d), jnp.bfloat16)],
        compiler_params=pltpu.CompilerParams(dimension_semantics=("arbitrary",)),
        name="out_proj",
    )(hg, da, w, x, nw.reshape(1, d))


SUB = 16
HG_SEQ_BLOCK = 512


def _split_dot(tri, a):
    hi = a.astype(jnp.bfloat16)
    lo = (a - hi.astype(jnp.float32)).astype(jnp.bfloat16)
    return (jnp.dot(tri, hi, preferred_element_type=jnp.float32)
            + jnp.dot(tri, lo, preferred_element_type=jnp.float32))


def _hgrn2_kernel(q_ref, f_ref, v_ref, g_ref, lb_ref, nw_ref, o_ref, state_t):
    @pl.when(pl.program_id(2) == 0)
    def _():
        state_t[...] = jnp.zeros_like(state_t)

    lp = lb_ref[...]
    le = jnp.exp(lp - jnp.max(lp, axis=0, keepdims=True))
    lb = le[0:1] / jnp.sum(le, axis=0, keepdims=True)
    nw = nw_ref[...]
    r64 =lax.broadcasted_iota(jnp.int32, (CHUNK, CHUNK), 0)
    c64 = lax.broadcasted_iota(jnp.int32, (CHUNK, CHUNK), 1)
    tri = (c64 <= r64).astype(jnp.bfloat16)
    row_k = lax.broadcasted_iota(jnp.int32, (CHUNK, HG_DK), 0)
    row_s = lax.broadcasted_iota(jnp.int32, (SUB, HG_DK), 0)
    lane_s = lax.broadcasted_iota(jnp.int32, (SUB, CHUNK), 1)

    def chunk(ci, carry):
        r0 = pl.multiple_of(ci * CHUNK, CHUNK)
        f = lb + (1.0 - lb) * jax.nn.sigmoid(f_ref[pl.ds(r0, CHUNK), :])
        k = 1.0 - f
        b = _split_dot(tri, jnp.log(f))
        q = q_ref[pl.ds(r0, CHUNK), :] * (HG_DK ** -0.5)
        v = v_ref[pl.ds(r0, CHUNK), :]
        strips = []
        for i in range(CHUNK // SUB):
            lo = i * SUB
            bi, qi, ki = b[lo:lo + SUB], q[lo:lo + SUB], k[lo:lo + SUB]
            diag = jnp.zeros((SUB, CHUNK), jnp.float32)
            for s in range(SUB):
                dec = jnp.exp(jnp.where(row_s >= s, bi - bi[s:s + 1], -jnp.inf))
                col = jnp.sum(qi * ki[s:s + 1] * dec, axis=-1, keepdims=True)
                diag = jnp.where(lane_s == lo + s, col, diag)
            if i == 0:
                strips.append(diag)
            else:
                ref = b[lo - 1:lo]
                qt = (qi * jnp.exp(bi - ref)).astype(jnp.bfloat16)
                kt = (k * jnp.exp(jnp.where(row_k < lo, ref - b, -jnp.inf))).astype(jnp.bfloat16)
                strips.append(diag + lax.dot_general(qt, kt, (((1,), (1,)), ((), ())),
                                                     preferred_element_type=jnp.float32))
        a = jnp.concatenate(strips, axis=0).astype(jnp.bfloat16)
        vb = v.astype(jnp.bfloat16)
        st = state_t[...]
        o = jnp.dot(a, vb, preferred_element_type=jnp.float32)
        o = o + lax.dot_general((q * jnp.exp(b)).astype(jnp.bfloat16), st.astype(jnp.bfloat16),
                                (((1,), (1,)), ((), ())), preferred_element_type=jnp.float32)
        b_last = b[CHUNK - 1:CHUNK]
        kd = (k * jnp.exp(b_last - b)).astype(jnp.bfloat16)
        state_t[...] = st * jnp.exp(b_last) + jnp.dot(v.T.astype(jnp.bfloat16), kd,
                                                      preferred_element_type=jnp.float32)
        y = o * lax.rsqrt(jnp.mean(o * o, axis=-1, keepdims=True) + RMS_EPS) * nw
        o_ref[pl.ds(r0, CHUNK), :] = (y * jax.nn.silu(g_ref[pl.ds(r0, CHUNK), :])).astype(o_ref.dtype)
        return carry

    lax.fori_loop(0, q_ref.shape[0] // CHUNK, chunk, 0)


def _hgrn2_pallas(proj, lb_param, norm_w, col0=0):
    bsz, seq, _ = proj.shape
    sb = min(HG_SEQ_BLOCK, seq)
    c0 = col0 // HG_DK
    nl = lb_param.shape[0]

    def col(group):
        return pl.BlockSpec((None, sb, HG_DK), lambda b, h, s: (b, s, c0 + group * HG_HEADS + h))

    return pl.pallas_call(
        _hgrn2_kernel,
        grid=(bsz, HG_HEADS, seq // sb),
        in_specs=[col(0), col(1), col(2), col(3),
                  pl.BlockSpec((nl, HG_DK), lambda b, h, s: (0, h)),
                  pl.BlockSpec((1, HG_DV), lambda b, h, s: (0, 0))],
        out_specs=pl.BlockSpec((None, sb, HG_DV), lambda b, h, s: (b, s, h)),
        out_shape=jax.ShapeDtypeStruct((bsz, seq, HG_HEADS * HG_DV), jnp.bfloat16),
        scratch_shapes=[pltpu.VMEM((HG_DV, HG_DK), jnp.float32)],
        compiler_params=pltpu.CompilerParams(dimension_semantics=("arbitrary", "arbitrary", "arbitrary")),
        name="hgrn2",
    )(proj, proj, proj, proj, lb_param, norm_w.reshape(1, HG_DV))


DA_TILE = 512
LAMBDA_INIT = 0.8 - 0.6 * math.exp(-0.3 * 0)


def _diff_attn_kernel(q_ref, k_ref, v_ref, lp_ref, nw_ref, o_ref, m_scr, l_scr, acc_scr):
    t = q_ref.shape[0]
    qi = pl.program_id(2)
    lane = lax.broadcasted_iota(jnp.int32, (t, 2 * DA_DH), 1)
    q = q_ref[...] * jnp.asarray(DA_DH ** -0.5, jnp.bfloat16)
    qs = [jnp.where(lane < DA_DH, q, jnp.zeros_like(q)), jnp.where(lane >= DA_DH, q, jnp.zeros_like(q))]
    m_scr[...] = jnp.full(m_scr.shape, -jnp.inf, jnp.float32)
    l_scr[...] = jnp.zeros_like(l_scr)
    acc_scr[...] = jnp.zeros_like(acc_scr)

    def sweep(kj, mask):
        k0 = pl.multiple_of(kj * t, t)
        kt = k_ref[pl.ds(k0, t), :]
        vt = v_ref[pl.ds(k0, t), :]
        for c in range(2):
            s = lax.dot_general(qs[c], kt, (((1,), (1,)), ((), ())), preferred_element_type=jnp.float32)
            if mask is not None:
                s = jnp.where(mask, s, -jnp.inf)
            m_old = m_scr[c]
            m_new = jnp.maximum(m_old, jnp.max(s, axis=-1, keepdims=True))
            alpha = jnp.exp(m_old - m_new)
            p = jnp.exp(s - jnp.tile(m_new, (1, t // 128)))
            l_scr[c] = alpha * l_scr[c] + jnp.sum(p, axis=-1, keepdims=True)
            acc_scr[c] = alpha * acc_scr[c] + jnp.dot(p.astype(jnp.bfloat16), vt,
                                                      preferred_element_type=jnp.float32)
            m_scr[c] = m_new

    def full_tile(kj, carry):
        sweep(kj, None)
        return carry

    lax.fori_loop(0, qi, full_tile, 0)
    row = lax.broadcasted_iota(jnp.int32, (t, t), 0)
    col = lax.broadcasted_iota(jnp.int32, (t, t), 1)
    sweep(qi, (col // CHUNK) <= (row // CHUNK))

    lp = lp_ref[...]
    lam = (jnp.exp(jnp.sum(lp[0:1] * lp[1:2], axis=-1, keepdims=True))
           - jnp.exp(jnp.sum(lp[2:3] * lp[3:4], axis=-1, keepdims=True)) + LAMBDA_INIT)
    o = acc_scr[0] / l_scr[0] - lam * (acc_scr[1] / l_scr[1])
    y = o * lax.rsqrt(jnp.mean(o * o, axis=-1, keepdims=True) + RMS_EPS) * nw_ref[...]
    o_ref[...] = (y * (1.0 - LAMBDA_INIT)).astype(o_ref.dtype)


def _diff_attn_pallas(q, k, v, lam_params, norm_w, cols=(0, 0, 0)):
    bsz, seq, _ = q.shape
    t = min(DA_TILE, seq)
    cq, ck, cv = cols
    return pl.pallas_call(
        _diff_attn_kernel,
        grid=(bsz, DA_HEADS, seq // t),
        in_specs=[pl.BlockSpec((None, t, 2 * DA_DH), lambda b, h, i: (b, i, cq + h)),
                  pl.BlockSpec((None, seq, 2 * DA_DH), lambda b, h, i: (b, 0, ck + h)),
                  pl.BlockSpec((None, seq, DA_DV), lambda b, h, i: (b, 0, cv + h)),
                  pl.BlockSpec((4, DA_DH), lambda b, h, i: (0, 0)),
                  pl.BlockSpec((1, DA_DV), lambda b, h, i: (0, 0))],
        out_specs=pl.BlockSpec((None, t, DA_DV), lambda b, h, i: (b, i, h)),
        out_shape=jax.ShapeDtypeStruct((bsz, seq, DA_HEADS * DA_DV), jnp.bfloat16),
        scratch_shapes=[pltpu.VMEM((2, t, 128), jnp.float32), pltpu.VMEM((2, t, 128), jnp.float32),
                        pltpu.VMEM((2, t, DA_DV), jnp.float32)],
        compiler_params=pltpu.CompilerParams(dimension_semantics=("arbitrary", "arbitrary", "arbitrary")),
        name="diff_attn",
    )(q, k, v, lam_params, norm_w.reshape(1, DA_DV))


PEER_N_EXPERTS = PEER_N_KEYS * PEER_N_KEYS
PEER_SEL = PEER_HEADS * PEER_TOPK
SLAB = D_MODEL // 2 // 128
HALF_D = D_MODEL // 2
_CAND_NB = [PEER_TOPK // (a + 1) for a in range(PEER_TOPK)]
_CAND_ROWS = 56


def _topk_rows(s, k):
    n = s.shape[0]
    iota = lax.broadcasted_iota(jnp.int32, s.shape, 0)
    vals, idxs = [], []
    for _ in range(k):
        m = jnp.max(s, axis=0, keepdims=True)
        idx = jnp.min(jnp.where(s == m, iota, n), axis=0, keepdims=True)
        s = jnp.where(iota == idx, -jnp.inf, s)
        vals.append(m)
        idxs.append(idx)
    return jnp.concatenate(vals, axis=0), jnp.concatenate(idxs, axis=0)


def _peer_topk_kernel(n2_ref, wqt_ref, sk_ref, idx_ref, g_ref, qh_scr):
    qh_scr[...] = lax.dot_general(wqt_ref[...], n2_ref[...], (((1,), (1,)), ((), ())),
                                  preferred_element_type=jnp.float32)

    def head(h, carry):
        base = pl.multiple_of(h * PEER_D_KEY, PEER_D_KEY)
        q1 = qh_scr[pl.ds(base, PEER_HALF), :].astype(jnp.bfloat16)
        q2 = qh_scr[pl.ds(base + PEER_HALF, PEER_HALF), :].astype(jnp.bfloat16)
        s1 = jnp.dot(sk_ref[h, 0], q1, preferred_element_type=jnp.float32)
        s2 = jnp.dot(sk_ref[h, 1], q2, preferred_element_type=jnp.float32)
        v1, i1 = _topk_rows(s1, PEER_TOPK)
        v2, i2 = _topk_rows(s2, PEER_TOPK)
        tb = v1.shape[1]
        cand = jnp.concatenate(
            [v1[a:a + 1] + v2[:_CAND_NB[a]] for a in range(PEER_TOPK)]
            + [jnp.full((_CAND_ROWS - sum(_CAND_NB), tb), -jnp.inf, jnp.float32)], axis=0)
        cidx = jnp.concatenate(
            [i1[a:a + 1] * PEER_N_KEYS + i2[:_CAND_NB[a]] for a in range(PEER_TOPK)]
            + [jnp.zeros((_CAND_ROWS - sum(_CAND_NB), tb), jnp.int32)], axis=0)
        ts, pos = _topk_rows(cand, PEER_TOPK)
        iota = lax.broadcasted_iota(jnp.int32, cand.shape, 0)
        eid = jnp.concatenate(
            [jnp.sum(jnp.where(iota == pos[r:r + 1], cidx, 0), axis=0, keepdims=True)
             for r in range(PEER_TOPK)], axis=0)
        e = jnp.exp(ts - ts[0:1])
        g = e / jnp.sum(e, axis=0, keepdims=True)
        row = pl.multiple_of(h * PEER_TOPK, PEER_TOPK)
        idx_ref[pl.ds(row, PEER_TOPK), :] = eid
        g_ref[pl.ds(row, PEER_TOPK), :] = g
        return carry

    lax.fori_loop(0, PEER_HEADS, head, 0)


def _peer_topk(n2b, wqt, sk, tb=256):
    n = n2b.shape[0]
    return pl.pallas_call(
        _peer_topk_kernel,
        grid=(n // tb,),
        in_specs=[pl.BlockSpec((tb, D_MODEL), lambda i: (i, 0)),
                  pl.BlockSpec(wqt.shape, lambda i: (0, 0)),
                  pl.BlockSpec(sk.shape, lambda i: (0, 0, 0, 0))],
        out_specs=[pl.BlockSpec((PEER_SEL, tb), lambda i: (0, i)),
                   pl.BlockSpec((PEER_SEL, tb), lambda i: (0, i))],
        out_shape=[jax.ShapeDtypeStruct((PEER_SEL, n), jnp.int32),
                   jax.ShapeDtypeStruct((PEER_SEL, n), jnp.float32)],
        scratch_shapes=[pltpu.VMEM((PEER_HEADS * PEER_D_KEY, tb), jnp.float32)],
        compiler_params=pltpu.CompilerParams(dimension_semantics=("arbitrary",),
                                             vmem_limit_bytes=40 * 1024 * 1024),
        name="peer_topk",
    )(n2b, wqt, sk)


def _pack_table(t):
    e = t.shape[0]
    tb = t.astype(jnp.bfloat16).reshape(e, 2, SLAB, 128)
    bits = lax.bitcast_convert_type(tb, jnp.uint16).astype(jnp.uint32)
    word = bits[:, 0] | (bits[:, 1] << 16)
    return lax.bitcast_convert_type(word, jnp.int32).reshape(e * SLAB, 128)


IDX_TOKENS = 32


def _index_copy(idx_hbm, ibuf, sem, blk, slot):
    return pltpu.make_async_copy(idx_hbm.at[blk], ibuf.at[slot], sem.at[slot])


def _for_both_index_buffers(idx_hbm, ibuf, sem, process):
    i = pl.program_id(0)

    @pl.when(i == 0)
    def _():
        _index_copy(idx_hbm, ibuf, sem, 0, 0).start()

    _index_copy(idx_hbm, ibuf, sem, 2 * i + 1, 1).start()
    _index_copy(idx_hbm, ibuf, sem, 2 * i, 0).wait()
    process(0)

    @pl.when(i + 1 < pl.num_programs(0))
    def _():
        _index_copy(idx_hbm, ibuf, sem, 2 * i + 2, 0).start()

    _index_copy(idx_hbm, ibuf, sem, 2 * i + 1, 1).wait()
    process(1)


def _gather_rows(ibuf, slot, base, tab_ref, tile):
    for mi in range(PEER_SEL):
        i = pl.multiple_of(ibuf[slot, base + mi], SLAB)
        tile[pl.ds(SLAB * mi, SLAB), :] = tab_ref[pl.ds(i, SLAB), :]


def _gathered_matrix(tile):
    return jnp.concatenate(
        [pltpu.bitcast(tile[pl.ds(j, PEER_SEL, stride=SLAB), :], jnp.bfloat16) for j in range(SLAB)],
        axis=-1)


GROUP = 8


def _peer_u_kernel(idx_hbm, tab_ref, x_ref, g_ref, w_ref, ibuf, sem, tile_a, tile_b, r_scr):
    def process(slot):
        for gi in range(IDX_TOKENS // GROUP):
            r0 = slot * IDX_TOKENS + gi * GROUP
            xg = x_ref[pl.ds(r0, GROUP), :]
            rs = []
            for k in range(GROUP):
                tile = tile_a if k % 2 == 0 else tile_b
                _gather_rows(ibuf, slot, (gi * GROUP + k) * PEER_SEL, tab_ref, tile)
                x2 = jnp.concatenate([xg[k:k + 1, :HALF_D], xg[k:k + 1, HALF_D:]], axis=0).astype(jnp.bfloat16)
                rs.append(lax.dot_general(x2, _gathered_matrix(tile), (((1,), (1,)), ((), ())),
                                          preferred_element_type=jnp.float32))
            r_scr[pl.ds(2 * r0, 2 * GROUP), :] = jnp.concatenate(rs, axis=0)

    _for_both_index_buffers(idx_hbm, ibuf, sem, process)

    e = r_scr[...]
    nl = e.shape[1]
    row = lax.broadcasted_iota(jnp.int32, e.shape, 0)
    lane = lax.broadcasted_iota(jnp.int32, e.shape, 1)
    nxt = pltpu.roll(pltpu.roll(e, e.shape[0] - 1, axis=0), nl - 1, axis=1)
    prv = pltpu.roll(pltpu.roll(e, 1, axis=0), 1, axis=1)
    act = e + jnp.where(row % 2 == 0, nxt, prv)
    w = jnp.where(row % 2 == lane % 2, g_ref[...] * jax.nn.gelu(act), 0.0)
    w_ref[...] = w.astype(jnp.bfloat16)


def _peer_v_kernel(idx_hbm, tab_ref, w_ref, h_ref, nw_ref, o_ref, ibuf, sem, tile_a, tile_b, r_scr):
    def process(slot):
        for gi in range(IDX_TOKENS // GROUP):
            r0 = slot * IDX_TOKENS + gi * GROUP
            wg = w_ref[pl.ds(2 * r0, 2 * GROUP), :]
            rs = []
            for k in range(GROUP):
                tile = tile_a if k % 2 == 0 else tile_b
                _gather_rows(ibuf, slot, (gi * GROUP + k) * PEER_SEL, tab_ref, tile)
                rs.append(jnp.dot(wg[2 * k:2 * k + 2], _gathered_matrix(tile),
                                  preferred_element_type=jnp.float32))
            r_scr[pl.ds(r0, GROUP), :HALF_D] = jnp.concatenate([r[0:1] for r in rs], axis=0)
            r_scr[pl.ds(r0, GROUP), HALF_D:] = jnp.concatenate([r[1:2] for r in rs], axis=0)

    _for_both_index_buffers(idx_hbm, ibuf, sem, process)

    y = h_ref[...] + r_scr[...]
    y = y * lax.rsqrt(jnp.mean(y * y, axis=-1, keepdims=True) + RMS_EPS)
    o_ref[...] = y * nw_ref[...]


_PEER_VMEM = 56 * 1024 * 1024


def _peer_u(idx_flat, tab, x, g2):
    tb = 2 * IDX_TOKENS
    n = x.shape[0]
    return pl.pallas_call(
        _peer_u_kernel,
        grid=(n // tb,),
        in_specs=[pl.BlockSpec(memory_space=pl.ANY),
                  pl.BlockSpec(memory_space=pltpu.VMEM),
                  pl.BlockSpec((tb, D_MODEL), lambda i: (i, 0)),
                  pl.BlockSpec((2 * tb, 2 * PEER_SEL), lambda i: (i, 0))],
        out_specs=pl.BlockSpec((2 * tb, 2 * PEER_SEL), lambda i: (i, 0)),
        out_shape=jax.ShapeDtypeStruct((2 * n, 2 * PEER_SEL), jnp.bfloat16),
        scratch_shapes=[pltpu.SMEM((2, IDX_TOKENS * PEER_SEL), jnp.int32),
                        pltpu.SemaphoreType.DMA((2,)),
                        pltpu.VMEM((PEER_SEL * SLAB, 128), jnp.int32),
                        pltpu.VMEM((PEER_SEL * SLAB, 128), jnp.int32),
                        pltpu.VMEM((2 * tb, 2 * PEER_SEL), jnp.float32)],
        compiler_params=pltpu.CompilerParams(dimension_semantics=("arbitrary",),
                                             vmem_limit_bytes=_PEER_VMEM),
        name="peer_u",
    )(idx_flat, tab, x, g2)


def _peer_v(idx_flat, tab, w, h, nw):
    tb = 2 * IDX_TOKENS
    n = h.shape[0]
    return pl.pallas_call(
        _peer_v_kernel,
        grid=(n // tb,),
        in_specs=[pl.BlockSpec(memory_space=pl.ANY),
                  pl.BlockSpec(memory_space=pltpu.VMEM),
                  pl.BlockSpec((2 * tb, 2 * PEER_SEL), lambda i: (i, 0)),
                  pl.BlockSpec((tb, D_MODEL), lambda i: (i, 0)),
                  pl.BlockSpec((1, D_MODEL), lambda i: (0, 0))],
        out_specs=pl.BlockSpec((tb, D_MODEL), lambda i: (i, 0)),
        out_shape=jax.ShapeDtypeStruct((n, D_MODEL), jnp.float32),
        scratch_shapes=[pltpu.SMEM((2, IDX_TOKENS * PEER_SEL), jnp.int32),
                        pltpu.SemaphoreType.DMA((2,)),
                        pltpu.VMEM((PEER_SEL * SLAB, 128), jnp.int32),
                        pltpu.VMEM((PEER_SEL * SLAB, 128), jnp.int32),
                        pltpu.VMEM((tb, D_MODEL), jnp.float32)],
        compiler_params=pltpu.CompilerParams(dimension_semantics=("arbitrary",),
                                             vmem_limit_bytes=_PEER_VMEM),
        name="peer_v",
    )(idx_flat, tab, w, h, nw)


def _peer_block(n2, n2b, h, w_query, sub_keys, u, v, norm_final_w):
    n = n2.shape[0]
    wqt = w_query.T.astype(jnp.bfloat16)
    sk = sub_keys.astype(jnp.bfloat16)
    idx_t, g_t = _peer_topk(n2b, wqt, sk)
    idx_flat = (idx_t.T * SLAB).reshape(n // IDX_TOKENS, IDX_TOKENS * PEER_SEL)
    g2 = jnp.repeat(jnp.repeat(g_t.T, 2, axis=1), 2, axis=0)
    w = _peer_u(idx_flat, _pack_table(u), n2, g2)
    return _peer_v(idx_flat, _pack_table(v), w, h, norm_final_w.reshape(1, D_MODEL))


HG_COLS = HG_HEADS * (2 * HG_DK + 2 * HG_DV)


def kernel(x, norm_mix_w, w_in, hgrn_lb_param, hgrn_norm_w, diff_lambda, diff_norm_w,
           w_out, norm_ffn_w, peer_w_query, peer_sub_keys, peer_u, peer_v, norm_final_w):
    bsz, seq, d = x.shape
    n = bsz * seq
    x2 = x.reshape(n, d)
    w_in_b = w_in[0].astype(jnp.bfloat16)
    proj_hg = _norm_proj(x2, norm_mix_w[0], w_in_b[:, :HG_COLS], jnp.float32).reshape(bsz, seq, HG_COLS)
    proj_da = _norm_proj(x2, norm_mix_w[0], w_in_b[:, HG_COLS:], jnp.bfloat16).reshape(bsz, seq, -1)
    hg_out = _hgrn2_pallas(proj_hg, hgrn_lb_param, hgrn_norm_w[0])
    da_out = _diff_attn_pallas(proj_da, proj_da, proj_da, diff_lambda[0], diff_norm_w[0],
                               cols=(0, DA_HEADS, 2 * DA_HEADS))
    h, n2, n2b = _out_proj(hg_out.reshape(n, -1), da_out.reshape(n, -1), w_out[0].astype(jnp.bfloat16),
                           x2, norm_ffn_w[0])
    out = _peer_block(n2, n2b, h, peer_w_query[0], peer_sub_keys[0], peer_u[0], peer_v[0], norm_final_w)
    return out.reshape(bsz, seq, d)
```

```python
import math
import jax, jax.numpy as jnp
from jax import lax
from jax.experimental import pallas as pl
from jax.experimental.pallas import tpu as pltpu

D_MODEL = 1024
BATCH = 8
SEQ = 4096
CHUNK = 64
HG_HEADS = 4
HG_DK = 128
HG_DV = 128
DA_HEADS = 4
DA_DV = 128
DA_DH = 64
Q_BLOCK = 128
PEER_HEADS = 8
PEER_N_KEYS = 128
PEER_D_KEY = 256
PEER_HALF = 128
PEER_TOPK = 16
TOKEN_BLOCK = 128
RMS_EPS = 1e-6


def _rmsnorm(x, w):
    y = x * lax.rsqrt(jnp.mean(x * x, axis=-1, keepdims=True) + RMS_EPS)
    return y * w


def _hgrn2(q, f_logit, inp, gate, lb, norm_w):
    B, S = q.shape[0], q.shape[1]
    n_chunks = S // CHUNK
    f = lb + (1.0 - lb) * jax.nn.sigmoid(f_logit)
    k = 1.0 - f
    log_f = jnp.log(f)
    qf = q * (HG_DK ** -0.5)

    def to_chunks(t, d):
        return t.reshape(B, n_chunks, CHUNK, HG_HEADS, d).transpose(1, 0, 3, 2, 4)

    qc, kc, ac = to_chunks(qf, HG_DK), to_chunks(k, HG_DK), to_chunks(log_f, HG_DK)
    vc = to_chunks(inp, HG_DV)
    tri = jnp.tril(jnp.ones((CHUNK, CHUNK), dtype=bool))

    def step(state, xs):
        qb, kb, vb, ab = xs
        b = jnp.cumsum(ab, axis=2)
        diff = b[:, :, :, None, :] - b[:, :, None, :, :]
        decay = jnp.exp(jnp.where(tri[:, :, None], diff, -jnp.inf))
        scores = jnp.einsum('bhtd,bhsd,bhtsd->bhts', qb, kb, decay)
        o = jnp.einsum('bhts,bhse->bhte', scores, vb) \
            + jnp.einsum('bhtd,bhde->bhte', qb * jnp.exp(b), state)
        b_last = b[:, :, -1]
        state = jnp.exp(b_last)[..., None] * state \
            + jnp.einsum('bhsd,bhse->bhde', kb * jnp.exp(b_last[:, :, None, :] - b), vb)
        return state, o

    s0 = jnp.zeros((B, HG_HEADS, HG_DK, HG_DV), jnp.float32)
    _, oc = lax.scan(step, s0, (qc, kc, vc, ac))
    o = oc.transpose(1, 0, 3, 2, 4).reshape(B, S, HG_HEADS, HG_DV)
    g = gate.reshape(B, S, HG_HEADS, HG_DV)
    o = _rmsnorm(o, norm_w) * jax.nn.silu(g)
    return o.reshape(B, S, HG_HEADS * HG_DV)


def _diff_attn(q, k, v, lam_params, norm_w):
    B, S = q.shape[0], q.shape[1]
    n_blocks = S // Q_BLOCK
    lam_init = 0.8 - 0.6 * math.exp(0.0)
    lp = lam_params
    lam = jnp.exp(jnp.sum(lp[0] * lp[1])) - jnp.exp(jnp.sum(lp[2] * lp[3])) + lam_init
    scale = DA_DH ** -0.5
    qf = q.reshape(B, n_blocks, Q_BLOCK, DA_HEADS, 2, DA_DH).transpose(1, 0, 2, 3, 4, 5)
    kf = k.reshape(B, S, DA_HEADS, 2, DA_DH)
    vf = v.reshape(B, S, DA_HEADS, DA_DV)
    key_chunk = jnp.arange(S) // CHUNK

    def one_block(args):
        qb, blk = args
        s = jnp.einsum('bqhcd,bkhcd->bhcqk', qb, kf) * scale
        q_chunk = (blk * Q_BLOCK + jnp.arange(Q_BLOCK)) // CHUNK
        mask = key_chunk[None, :] <= q_chunk[:, None]
        p = jax.nn.softmax(jnp.where(mask, s, -jnp.inf), axis=-1)
        a = p[:, :, 0] - lam * p[:, :, 1]
        return jnp.einsum('bhqk,bkhe->bqhe', a, vf)

    o = lax.map(one_block, (qf, jnp.arange(n_blocks)))
    o = o.transpose(1, 0, 2, 3, 4).reshape(B, S, DA_HEADS, DA_DV)
    o = _rmsnorm(o, norm_w) * (1.0 - lam_init)
    return o.reshape(B, S, DA_HEADS * DA_DV)


PROJ_TM = 512
PROJ_TN = 512


def _norm_proj_kernel(x_ref, nw_ref, w_ref, o_ref, n_scr):
    @pl.when(pl.program_id(1) == 0)
    def _():
        x = x_ref[...]
        y = x * lax.rsqrt(jnp.mean(x * x, axis=-1, keepdims=True) + RMS_EPS) * nw_ref[...]
        n_scr[...] = y.astype(jnp.bfloat16)

    o_ref[...] = jnp.dot(n_scr[...], w_ref[...], preferred_element_type=jnp.float32).astype(o_ref.dtype)


def _norm_proj(x, nw, w, out_dtype):
    n, d = x.shape
    e = w.shape[1]
    tm, tn = min(PROJ_TM, n), min(PROJ_TN, e)
    return pl.pallas_call(
        _norm_proj_kernel,
        grid=(n // tm, e // tn),
        in_specs=[pl.BlockSpec((tm, d), lambda i, j: (i, 0)),
                  pl.BlockSpec((1, d), lambda i, j: (0, 0)),
                  pl.BlockSpec((d, tn), lambda i, j: (0, j))],
        out_specs=pl.BlockSpec((tm, tn), lambda i, j: (i, j)),
        out_shape=jax.ShapeDtypeStruct((n, e), out_dtype),
        scratch_shapes=[pltpu.VMEM((tm, d), jnp.bfloat16)],
        compiler_params=pltpu.CompilerParams(dimension_semantics=("arbitrary", "arbitrary")),
        name="norm_proj",
    )(x, nw.reshape(1, d), w)


def _out_proj_kernel(hg_ref, da_ref, w_ref, x_ref, nw_ref, h_ref, n_ref, nb_ref):
    half = hg_ref.shape[1]
    h = (x_ref[...]
         + jnp.dot(hg_ref[...], w_ref[pl.ds(0, half), :], preferred_element_type=jnp.float32)
         + jnp.dot(da_ref[...], w_ref[pl.ds(half, half), :], preferred_element_type=jnp.float32))
    h_ref[...] = h
    n = h * lax.rsqrt(jnp.mean(h * h, axis=-1, keepdims=True) + RMS_EPS) * nw_ref[...]
    n_ref[...] = n
    nb_ref[...] = n.astype(jnp.bfloat16)


def _out_proj(hg, da, w, x, nw):
    n, d = x.shape
    half = hg.shape[1]
    tm = min(PROJ_TM, n)
    row = lambda cols: pl.BlockSpec((tm, cols), lambda i: (i, 0))
    return pl.pallas_call(
        _out_proj_kernel,
        grid=(n // tm,),
        in_specs=[row(half), row(half), pl.BlockSpec((2 * half, d), lambda i: (0, 0)), row(d),
                  pl.BlockSpec((1, d), lambda i: (0, 0))],
        out_specs=[row(d), row(d), row(d)],
        out_shape=[jax.ShapeDtypeStruct((n, d), jnp.float32), jax.ShapeDtypeStruct((n, d), jnp.float32),
                   jax.ShapeDtypeStruct((n, d), jnp.bfloat16)],
        compiler_params=pltpu.CompilerParams(dimension_semantics=("arbitrary",)),
        name="out_proj",
    )(hg, da, w, x, nw.reshape(1, d))


SUB = 16
HG_SEQ_BLOCK = 512


def _split_dot(tri, a):
    hi = a.astype(jnp.bfloat16)
    lo = (a - hi.astype(jnp.float32)).astype(jnp.bfloat16)
    return (jnp.dot(tri, hi, preferred_element_type=jnp.float32)
            + jnp.dot(tri, lo, preferred_element_type=jnp.float32))


def _hgrn2_kernel(q_ref, f_ref, v_ref, g_ref, lb_ref, nw_ref, o_ref, state_t):
    @pl.when(pl.program_id(2) == 0)
    def _():
        state_t[...] = jnp.zeros_like(state_t)

    lp = lb_ref[...]
    le = jnp.exp(lp - jnp.max(lp, axis=0, keepdims=True))
    lb = le[0:1] / jnp.sum(le, axis=0, keepdims=True)
    nw = nw_ref[...]
    r64 =lax.broadcasted_iota(jnp.int32, (CHUNK, CHUNK), 0)
    c64 = lax.broadcasted_iota(jnp.int32, (CHUNK, CHUNK), 1)
    tri = (c64 <= r64).astype(jnp.bfloat16)
    row_k = lax.broadcasted_iota(jnp.int32, (CHUNK, HG_DK), 0)
    row_s = lax.broadcasted_iota(jnp.int32, (SUB, HG_DK), 0)
    lane_s = lax.broadcasted_iota(jnp.int32, (SUB, CHUNK), 1)

    def chunk(ci, carry):
        r0 = pl.multiple_of(ci * CHUNK, CHUNK)
        f = lb + (1.0 - lb) * jax.nn.sigmoid(f_ref[pl.ds(r0, CHUNK), :])
        k = 1.0 - f
        b = _split_dot(tri, jnp.log(f))
        q = q_ref[pl.ds(r0, CHUNK), :] * (HG_DK ** -0.5)
        v = v_ref[pl.ds(r0, CHUNK), :]
        strips = []
        for i in range(CHUNK // SUB):
            lo = i * SUB
            bi, qi, ki = b[lo:lo + SUB], q[lo:lo + SUB], k[lo:lo + SUB]
            diag = jnp.zeros((SUB, CHUNK), jnp.float32)
            for s in range(SUB):
                dec = jnp.exp(jnp.where(row_s >= s, bi - bi[s:s + 1], -jnp.inf))
                col = jnp.sum(qi * ki[s:s + 1] * dec, axis=-1, keepdims=True)
                diag = jnp.where(lane_s == lo + s, col, diag)
            if i == 0:
                strips.append(diag)
            else:
                ref = b[lo - 1:lo]
                qt = (qi * jnp.exp(bi - ref)).astype(jnp.bfloat16)
                kt = (k * jnp.exp(jnp.where(row_k < lo, ref - b, -jnp.inf))).astype(jnp.bfloat16)
                strips.append(diag + lax.dot_general(qt, kt, (((1,), (1,)), ((), ())),
                                                     preferred_element_type=jnp.float32))
        a = jnp.concatenate(strips, axis=0).astype(jnp.bfloat16)
        vb = v.astype(jnp.bfloat16)
        st = state_t[...]
        o = jnp.dot(a, vb, preferred_element_type=jnp.float32)
        o = o + lax.dot_general((q * jnp.exp(b)).astype(jnp.bfloat16), st.astype(jnp.bfloat16),
                                (((1,), (1,)), ((), ())), preferred_element_type=jnp.float32)
        b_last = b[CHUNK - 1:CHUNK]
        kd = (k * jnp.exp(b_last - b)).astype(jnp.bfloat16)
        state_t[...] = st * jnp.exp(b_last) + jnp.dot(v.T.astype(jnp.bfloat16), kd,
                                                      preferred_element_type=jnp.float32)
        y = o * lax.rsqrt(jnp.mean(o * o, axis=-1, keepdims=True) + RMS_EPS) * nw
        o_ref[pl.ds(r0, CHUNK), :] = (y * jax.nn.silu(g_ref[pl.ds(r0, CHUNK), :])).astype(o_ref.dtype)
        return carry

    lax.fori_loop(0, q_ref.shape[0] // CHUNK, chunk, 0)


def _hgrn2_pallas(proj, lb_param, norm_w, col0=0):
    bsz, seq, _ = proj.shape
    sb = min(HG_SEQ_BLOCK, seq)
    c0 = col0 // HG_DK
    nl = lb_param.shape[0]

    def col(group):
        return pl.BlockSpec((None, sb, HG_DK), lambda b, h, s: (b, s, c0 + group * HG_HEADS + h))

    return pl.pallas_call(
        _hgrn2_kernel,
        grid=(bsz, HG_HEADS, seq // sb),
        in_specs=[col(0), col(1), col(2), col(3),
                  pl.BlockSpec((nl, HG_DK), lambda b, h, s: (0, h)),
                  pl.BlockSpec((1, HG_DV), lambda b, h, s: (0, 0))],
        out_specs=pl.BlockSpec((None, sb, HG_DV), lambda b, h, s: (b, s, h)),
        out_shape=jax.ShapeDtypeStruct((bsz, seq, HG_HEADS * HG_DV), jnp.bfloat16),
        scratch_shapes=[pltpu.VMEM((HG_DV, HG_DK), jnp.float32)],
        compiler_params=pltpu.CompilerParams(dimension_semantics=("arbitrary", "arbitrary", "arbitrary")),
        name="hgrn2",
    )(proj, proj, proj, proj, lb_param, norm_w.reshape(1, HG_DV))


DA_TILE = 512
LAMBDA_INIT = 0.8 - 0.6 * math.exp(-0.3 * 0)


def _diff_attn_kernel(q_ref, k_ref, v_ref, lp_ref, nw_ref, o_ref, m_scr, l_scr, acc_scr):
    t = q_ref.shape[0]
    qi = pl.program_id(2)
    lane = lax.broadcasted_iota(jnp.int32, (t, 2 * DA_DH), 1)
    q = q_ref[...] * jnp.asarray(DA_DH ** -0.5, jnp.bfloat16)
    qs = [jnp.where(lane < DA_DH, q, jnp.zeros_like(q)), jnp.where(lane >= DA_DH, q, jnp.zeros_like(q))]
    m_scr[...] = jnp.full(m_scr.shape, -jnp.inf, jnp.float32)
    l_scr[...] = jnp.zeros_like(l_scr)
    acc_scr[...] = jnp.zeros_like(acc_scr)

    def sweep(kj, mask):
        k0 = pl.multiple_of(kj * t, t)
        kt = k_ref[pl.ds(k0, t), :]
        vt = v_ref[pl.ds(k0, t), :]
        for c in range(2):
            s = lax.dot_general(qs[c], kt, (((1,), (1,)), ((), ())), preferred_element_type=jnp.float32)
            if mask is not None:
                s = jnp.where(mask, s, -jnp.inf)
            m_old = m_scr[c]
            m_new = jnp.maximum(m_old, jnp.max(s, axis=-1, keepdims=True))
            alpha = jnp.exp(m_old - m_new)
            p = jnp.exp(s - jnp.tile(m_new, (1, t // 128)))
            l_scr[c] = alpha * l_scr[c] + jnp.sum(p, axis=-1, keepdims=True)
            acc_scr[c] = alpha * acc_scr[c] + jnp.dot(p.astype(jnp.bfloat16), vt,
                                                      preferred_element_type=jnp.float32)
            m_scr[c] = m_new

    def full_tile(kj, carry):
        sweep(kj, None)
        return carry

    lax.fori_loop(0, qi, full_tile, 0)
    row = lax.broadcasted_iota(jnp.int32, (t, t), 0)
    col = lax.broadcasted_iota(jnp.int32, (t, t), 1)
    sweep(qi, (col // CHUNK) <= (row // CHUNK))

    lp = lp_ref[...]
    lam = (jnp.exp(jnp.sum(lp[0:1] * lp[1:2], axis=-1, keepdims=True))
           - jnp.exp(jnp.sum(lp[2:3] * lp[3:4], axis=-1, keepdims=True)) + LAMBDA_INIT)
    o = acc_scr[0] / l_scr[0] - lam * (acc_scr[1] / l_scr[1])
    y = o * lax.rsqrt(jnp.mean(o * o, axis=-1, keepdims=True) + RMS_EPS) * nw_ref[...]
    o_ref[...] = (y * (1.0 - LAMBDA_INIT)).astype(o_ref.dtype)


def _diff_attn_pallas(q, k, v, lam_params, norm_w, cols=(0, 0, 0)):
    bsz, seq, _ = q.shape
    t = min(DA_TILE, seq)
    cq, ck, cv = cols
    return pl.pallas_call(
        _diff_attn_kernel,
        grid=(bsz, DA_HEADS, seq // t),
        in_specs=[pl.BlockSpec((None, t, 2 * DA_DH), lambda b, h, i: (b, i, cq + h)),
                  pl.BlockSpec((None, seq, 2 * DA_DH), lambda b, h, i: (b, 0, ck + h)),
                  pl.BlockSpec((None, seq, DA_DV), lambda b, h, i: (b, 0, cv + h)),
                  pl.BlockSpec((4, DA_DH), lambda b, h, i: (0, 0)),
                  pl.BlockSpec((1, DA_DV), lambda b, h, i: (0, 0))],
        out_specs=pl.BlockSpec((None, t, DA_DV), lambda b, h, i: (b, i, h)),
        out_shape=jax.ShapeDtypeStruct((bsz, seq, DA_HEADS * DA_DV), jnp.bfloat16),
        scratch_shapes=[pltpu.VMEM((2, t, 128), jnp.float32), pltpu.VMEM((2, t, 128), jnp.float32),
                        pltpu.VMEM((2, t, DA_DV), jnp.float32)],
        compiler_params=pltpu.CompilerParams(dimension_semantics=("arbitrary", "arbitrary", "arbitrary")),
        name="diff_attn",
    )(q, k, v, lam_params, norm_w.reshape(1, DA_DV))


PEER_N_EXPERTS = PEER_N_KEYS * PEER_N_KEYS
PEER_SEL = PEER_HEADS * PEER_TOPK
SLAB = D_MODEL // 2 // 128
HALF_D = D_MODEL // 2
_CAND_NB = [PEER_TOPK // (a + 1) for a in range(PEER_TOPK)]
_CAND_ROWS = 56


def _topk_rows(s, k):
    n = s.shape[0]
    iota = lax.broadcasted_iota(jnp.int32, s.shape, 0)
    vals, idxs = [], []
    for _ in range(k):
        m = jnp.max(s, axis=0, keepdims=True)
        idx = jnp.min(jnp.where(s == m, iota, n), axis=0, keepdims=True)
        s = jnp.where(iota == idx, -jnp.inf, s)
        vals.append(m)
        idxs.append(idx)
    return jnp.concatenate(vals, axis=0), jnp.concatenate(idxs, axis=0)


def _peer_topk_kernel(n2_ref, wqt_ref, sk_ref, idx_ref, g_ref, qh_scr):
    qh_scr[...] = lax.dot_general(wqt_ref[...], n2_ref[...], (((1,), (1,)), ((), ())),
                                  preferred_element_type=jnp.float32)

    def head(h, carry):
        base = pl.multiple_of(h * PEER_D_KEY, PEER_D_KEY)
        q1 = qh_scr[pl.ds(base, PEER_HALF), :].astype(jnp.bfloat16)
        q2 = qh_scr[pl.ds(base + PEER_HALF, PEER_HALF), :].astype(jnp.bfloat16)
        s1 = jnp.dot(sk_ref[h, 0], q1, preferred_element_type=jnp.float32)
        s2 = jnp.dot(sk_ref[h, 1], q2, preferred_element_type=jnp.float32)
        v1, i1 = _topk_rows(s1, PEER_TOPK)
        v2, i2 = _topk_rows(s2, PEER_TOPK)
        tb = v1.shape[1]
        cand = jnp.concatenate(
            [v1[a:a + 1] + v2[:_CAND_NB[a]] for a in range(PEER_TOPK)]
            + [jnp.full((_CAND_ROWS - sum(_CAND_NB), tb), -jnp.inf, jnp.float32)], axis=0)
        cidx = jnp.concatenate(
            [i1[a:a + 1] * PEER_N_KEYS + i2[:_CAND_NB[a]] for a in range(PEER_TOPK)]
            + [jnp.zeros((_CAND_ROWS - sum(_CAND_NB), tb), jnp.int32)], axis=0)
        ts, pos = _topk_rows(cand, PEER_TOPK)
        iota = lax.broadcasted_iota(jnp.int32, cand.shape, 0)
        eid = jnp.concatenate(
            [jnp.sum(jnp.where(iota == pos[r:r + 1], cidx, 0), axis=0, keepdims=True)
             for r in range(PEER_TOPK)], axis=0)
        e = jnp.exp(ts - ts[0:1])
        g = e / jnp.sum(e, axis=0, keepdims=True)
        row = pl.multiple_of(h * PEER_TOPK, PEER_TOPK)
        idx_ref[pl.ds(row, PEER_TOPK), :] = eid
        g_ref[pl.ds(row, PEER_TOPK), :] = g
        return carry

    lax.fori_loop(0, PEER_HEADS, head, 0)


def _peer_topk(n2b, wqt, sk, tb=256):
    n = n2b.shape[0]
    return pl.pallas_call(
        _peer_topk_kernel,
        grid=(n // tb,),
        in_specs=[pl.BlockSpec((tb, D_MODEL), lambda i: (i, 0)),
                  pl.BlockSpec(wqt.shape, lambda i: (0, 0)),
                  pl.BlockSpec(sk.shape, lambda i: (0, 0, 0, 0))],
        out_specs=[pl.BlockSpec((PEER_SEL, tb), lambda i: (0, i)),
                   pl.BlockSpec((PEER_SEL, tb), lambda i: (0, i))],
        out_shape=[jax.ShapeDtypeStruct((PEER_SEL, n), jnp.int32),
                   jax.ShapeDtypeStruct((PEER_SEL, n), jnp.float32)],
        scratch_shapes=[pltpu.VMEM((PEER_HEADS * PEER_D_KEY, tb), jnp.float32)],
        compiler_params=pltpu.CompilerParams(dimension_semantics=("arbitrary",),
                                             vmem_limit_bytes=40 * 1024 * 1024),
        name="peer_topk",
    )(n2b, wqt, sk)


def _pack_table(t):
    e = t.shape[0]
    tb = t.astype(jnp.bfloat16).reshape(e, 2, SLAB, 128)
    bits = lax.bitcast_convert_type(tb, jnp.uint16).astype(jnp.uint32)
    word = jnp.pad(bits[:, 0] | (bits[:, 1] << 16), ((1, 1), (0, 0), (0, 0)))
    return lax.bitcast_convert_type(word, jnp.int32).reshape((e + 2) * SLAB, 128)


IDX_TOKENS = 32


def _index_copy(idx_hbm, ibuf, sem, blk, slot):
    return pltpu.make_async_copy(idx_hbm.at[blk], ibuf.at[slot], sem.at[slot])


def _for_both_index_buffers(idx_hbm, ibuf, sem, process):
    i = pl.program_id(0)

    @pl.when(i == 0)
    def _():
        _index_copy(idx_hbm, ibuf, sem, 0, 0).start()

    _index_copy(idx_hbm, ibuf, sem, 2 * i + 1, 1).start()
    _index_copy(idx_hbm, ibuf, sem, 2 * i, 0).wait()
    process(0)

    @pl.when(i + 1 < pl.num_programs(0))
    def _():
        _index_copy(idx_hbm, ibuf, sem, 2 * i + 2, 0).start()

    _index_copy(idx_hbm, ibuf, sem, 2 * i + 1, 1).wait()
    process(1)


def _gather_rows(ibuf, slot, base, tab_ref, tile):
    sub = lax.broadcasted_iota(jnp.int32, (2 * SLAB, 128), 0)
    for p in range(PEER_SEL // 2):
        ia = pl.multiple_of(ibuf[slot, base + 2 * p], SLAB)
        ib = pl.multiple_of(ibuf[slot, base + 2 * p + 1], SLAB)
        a = tab_ref[pl.ds(ia, 2 * SLAB), :]
        b = tab_ref[pl.ds(ib - SLAB, 2 * SLAB), :]
        tile[pl.ds(2 * SLAB * p, 2 * SLAB), :] = jnp.where(sub < SLAB, a, b)


def _gathered_matrix(tile):
    return jnp.concatenate(
        [pltpu.bitcast(tile[pl.ds(j, PEER_SEL, stride=SLAB), :], jnp.bfloat16) for j in range(SLAB)],
        axis=-1)


GROUP = 8
GATHER_TILES = 4


def _peer_u_kernel(idx_hbm, tab_ref, x_ref, g_ref, w_ref, ibuf, sem, *scratch):
    tiles, r_scr = scratch[:-1], scratch[-1]
    def process(slot):
        for gi in range(IDX_TOKENS // GROUP):
            r0 = slot * IDX_TOKENS + gi * GROUP
            xg = x_ref[pl.ds(r0, GROUP), :]
            rs = []
            for k in range(GROUP):
                tile = tiles[k % len(tiles)]
                _gather_rows(ibuf, slot, (gi * GROUP + k) * PEER_SEL, tab_ref, tile)
                x2 = jnp.concatenate([xg[k:k + 1, :HALF_D], xg[k:k + 1, HALF_D:]], axis=0).astype(jnp.bfloat16)
                rs.append(lax.dot_general(x2, _gathered_matrix(tile), (((1,), (1,)), ((), ())),
                                          preferred_element_type=jnp.float32))
            r_scr[pl.ds(2 * r0, 2 * GROUP), :] = jnp.concatenate(rs, axis=0)

    _for_both_index_buffers(idx_hbm, ibuf, sem, process)

    e = r_scr[...]
    nl = e.shape[1]
    row = lax.broadcasted_iota(jnp.int32, e.shape, 0)
    lane = lax.broadcasted_iota(jnp.int32, e.shape, 1)
    nxt = pltpu.roll(pltpu.roll(e, e.shape[0] - 1, axis=0), nl - 1, axis=1)
    prv = pltpu.roll(pltpu.roll(e, 1, axis=0), 1, axis=1)
    act = e + jnp.where(row % 2 == 0, nxt, prv)
    w = jnp.where(row % 2 == lane % 2, g_ref[...] * jax.nn.gelu(act), 0.0)
    w_ref[...] = w.astype(jnp.bfloat16)


def _peer_v_kernel(idx_hbm, tab_ref, w_ref, h_ref, nw_ref, o_ref, ibuf, sem, *scratch):
    tiles, r_scr = scratch[:-1], scratch[-1]
    def process(slot):
        for gi in range(IDX_TOKENS // GROUP):
            r0 = slot * IDX_TOKENS + gi * GROUP
            wg = w_ref[pl.ds(2 * r0, 2 * GROUP), :]
            rs = []
            for k in range(GROUP):
                tile = tiles[k % len(tiles)]
                _gather_rows(ibuf, slot, (gi * GROUP + k) * PEER_SEL, tab_ref, tile)
                rs.append(jnp.dot(wg[2 * k:2 * k + 2], _gathered_matrix(tile),
                                  preferred_element_type=jnp.float32))
            r_scr[pl.ds(r0, GROUP), :HALF_D] = jnp.concatenate([r[0:1] for r in rs], axis=0)
            r_scr[pl.ds(r0, GROUP), HALF_D:] = jnp.concatenate([r[1:2] for r in rs], axis=0)

    _for_both_index_buffers(idx_hbm, ibuf, sem, process)

    y = h_ref[...] + r_scr[...]
    y = y * lax.rsqrt(jnp.mean(y * y, axis=-1, keepdims=True) + RMS_EPS)
    o_ref[...] = y * nw_ref[...]


_PEER_VMEM = 56 * 1024 * 1024


def _peer_u(idx_flat, tab, x, g2):
    tb = 2 * IDX_TOKENS
    n = x.shape[0]
    return pl.pallas_call(
        _peer_u_kernel,
        grid=(n // tb,),
        in_specs=[pl.BlockSpec(memory_space=pl.ANY),
                  pl.BlockSpec(memory_space=pltpu.VMEM),
                  pl.BlockSpec((tb, D_MODEL), lambda i: (i, 0)),
                  pl.BlockSpec((2 * tb, 2 * PEER_SEL), lambda i: (i, 0))],
        out_specs=pl.BlockSpec((2 * tb, 2 * PEER_SEL), lambda i: (i, 0)),
        out_shape=jax.ShapeDtypeStruct((2 * n, 2 * PEER_SEL), jnp.bfloat16),
        scratch_shapes=[pltpu.SMEM((2, IDX_TOKENS * PEER_SEL), jnp.int32),
                        pltpu.SemaphoreType.DMA((2,)),
                        *[pltpu.VMEM((PEER_SEL * SLAB, 128), jnp.int32)] * GATHER_TILES,
                        pltpu.VMEM((2 * tb, 2 * PEER_SEL), jnp.float32)],
        compiler_params=pltpu.CompilerParams(dimension_semantics=("arbitrary",),
                                             vmem_limit_bytes=_PEER_VMEM),
        name="peer_u",
    )(idx_flat, tab, x, g2)


def _peer_v(idx_flat, tab, w, h, nw):
    tb = 2 * IDX_TOKENS
    n = h.shape[0]
    return pl.pallas_call(
        _peer_v_kernel,
        grid=(n // tb,),
        in_specs=[pl.BlockSpec(memory_space=pl.ANY),
                  pl.BlockSpec(memory_space=pltpu.VMEM),
                  pl.BlockSpec((2 * tb, 2 * PEER_SEL), lambda i: (i, 0)),
                  pl.BlockSpec((tb, D_MODEL), lambda i: (i, 0)),
                  pl.BlockSpec((1, D_MODEL), lambda i: (0, 0))],
        out_specs=pl.BlockSpec((tb, D_MODEL), lambda i: (i, 0)),
        out_shape=jax.ShapeDtypeStruct((n, D_MODEL), jnp.float32),
        scratch_shapes=[pltpu.SMEM((2, IDX_TOKENS * PEER_SEL), jnp.int32),
                        pltpu.SemaphoreType.DMA((2,)),
                        *[pltpu.VMEM((PEER_SEL * SLAB, 128), jnp.int32)] * GATHER_TILES,
                        pltpu.VMEM((tb, D_MODEL), jnp.float32)],
        compiler_params=pltpu.CompilerParams(dimension_semantics=("arbitrary",),
                                             vmem_limit_bytes=_PEER_VMEM),
        name="peer_v",
    )(idx_flat, tab, w, h, nw)


def _gather_matrix(ibuf, slot, base, tab_ref):
    sub = lax.broadcasted_iota(jnp.int32, (2 * SLAB, 128), 0)
    pairs = []
    for p in range(PEER_SEL // 2):
        ia = pl.multiple_of(ibuf[slot, base + 2 * p], SLAB)
        ib = pl.multiple_of(ibuf[slot, base + 2 * p + 1], SLAB)
        a = tab_ref[pl.ds(ia, 2 * SLAB), :]
        b = tab_ref[pl.ds(ib - SLAB, 2 * SLAB), :]
        pairs.append(jnp.where(sub < SLAB, a, b))
    return pltpu.bitcast(jnp.concatenate(pairs, axis=0), jnp.bfloat16)


ROWS = 2 * SLAB
GROUP = 8
GATHER_COLS = PEER_SEL * ROWS


def _split_hi_lo(a):
    hi = a.astype(jnp.bfloat16)
    return hi, (a - hi.astype(jnp.float32)).astype(jnp.bfloat16)


def _own_row_mask():
    r = lax.broadcasted_iota(jnp.int32, (ROWS, GATHER_COLS), 0)
    c = lax.broadcasted_iota(jnp.int32, (ROWS, GATHER_COLS), 1)
    return (c % ROWS) == r


def _peer_u_kernel(idx_hbm, tab_ref, xt_ref, g_ref, w_ref, ibuf, sem, p_scr):
    own = _own_row_mask()

    def process(slot):
        for gi in range(IDX_TOKENS // GROUP):
            r0 = slot * IDX_TOKENS + gi * GROUP
            parts = []
            for k in range(GROUP):
                t = _gather_matrix(ibuf, slot, (gi * GROUP + k) * PEER_SEL, tab_ref)
                x8 = xt_ref[pl.ds(ROWS * (r0 + k), ROWS), :].astype(jnp.bfloat16)
                r = lax.dot_general(x8, t, (((1,), (1,)), ((), ())), preferred_element_type=jnp.float32)
                parts.append(jnp.sum(jnp.where(own, r, 0.0), axis=0, keepdims=True))
            p_scr[pl.ds(r0, GROUP), :] = jnp.concatenate(parts, axis=0)

    _for_both_index_buffers(idx_hbm, ibuf, sem, process)

    col = lax.broadcasted_iota(jnp.int32, (GATHER_COLS, PEER_SEL), 0)
    exp = lax.broadcasted_iota(jnp.int32, (GATHER_COLS, PEER_SEL), 1)
    fold = ((col // ROWS) == exp).astype(jnp.bfloat16)
    hi, lo = _split_hi_lo(p_scr[...])
    act = (jnp.dot(hi, fold, preferred_element_type=jnp.float32)
           + jnp.dot(lo, fold, preferred_element_type=jnp.float32))
    hi, lo = _split_hi_lo(g_ref[...] * jax.nn.gelu(act))
    nt = (((1,), (1,)), ((), ()))
    w_ref[...] = (lax.dot_general(hi, fold, nt, preferred_element_type=jnp.float32)
                  + lax.dot_general(lo, fold, nt, preferred_element_type=jnp.float32))


RES_STRIDE = 72


def _peer_v_kernel(idx_hbm, tab_ref, w_ref, h_ref, nw_ref, o_ref, ibuf, sem, res):
    own = _own_row_mask()
    tb = h_ref.shape[0]

    def process(slot):
        for gi in range(IDX_TOKENS // GROUP):
            r0 = slot * IDX_TOKENS + gi * GROUP
            wg = w_ref[pl.ds(r0, GROUP), :]
            for k in range(GROUP):
                t = _gather_matrix(ibuf, slot, (gi * GROUP + k) * PEER_SEL, tab_ref)
                w8 = jnp.where(own, wg[k:k + 1], 0.0).astype(jnp.bfloat16)
                res[pl.ds(r0 + k, ROWS, stride=RES_STRIDE), :] = jnp.dot(w8, t, preferred_element_type=jnp.float32)

    _for_both_index_buffers(idx_hbm, ibuf, sem, process)

    planes = [res[pl.ds((2 * (c % SLAB) + c // SLAB) * RES_STRIDE, tb), :] for c in range(ROWS)]
    y = h_ref[...] + jnp.concatenate(planes, axis=-1)
    y = y * lax.rsqrt(jnp.mean(y * y, axis=-1, keepdims=True) + RMS_EPS)
    o_ref[...] = y * nw_ref[...]


def _peer_u(idx_rows, tab, xt, g):
    tb = 2 * IDX_TOKENS
    n = g.shape[0]
    return pl.pallas_call(
        _peer_u_kernel,
        grid=(n // tb,),
        in_specs=[pl.BlockSpec(memory_space=pl.ANY),
                  pl.BlockSpec(memory_space=pltpu.VMEM),
                  pl.BlockSpec((tb * ROWS, 128), lambda i: (i, 0)),
                  pl.BlockSpec((tb, PEER_SEL), lambda i: (i, 0))],
        out_specs=pl.BlockSpec((tb, GATHER_COLS), lambda i: (i, 0)),
        out_shape=jax.ShapeDtypeStruct((n, GATHER_COLS), jnp.float32),
        scratch_shapes=[pltpu.SMEM((2, IDX_TOKENS * PEER_SEL), jnp.int32),
                        pltpu.SemaphoreType.DMA((2,)),
                        pltpu.VMEM((tb, GATHER_COLS), jnp.float32)],
        compiler_params=pltpu.CompilerParams(dimension_semantics=("arbitrary",),
                                             vmem_limit_bytes=_PEER_VMEM),
        name="peer_u",
    )(idx_rows, tab, xt, g)


def _peer_v(idx_rows, tab, w, h, nw):
    tb = 2 * IDX_TOKENS
    n = h.shape[0]
    return pl.pallas_call(
        _peer_v_kernel,
        grid=(n // tb,),
        in_specs=[pl.BlockSpec(memory_space=pl.ANY),
                  pl.BlockSpec(memory_space=pltpu.VMEM),
                  pl.BlockSpec((tb, GATHER_COLS), lambda i: (i, 0)),
                  pl.BlockSpec((tb, D_MODEL), lambda i: (i, 0)),
                  pl.BlockSpec((1, D_MODEL), lambda i: (0, 0))],
        out_specs=pl.BlockSpec((tb, D_MODEL), lambda i: (i, 0)),
        out_shape=jax.ShapeDtypeStruct((n, D_MODEL), jnp.float32),
        scratch_shapes=[pltpu.SMEM((2, IDX_TOKENS * PEER_SEL), jnp.int32),
                        pltpu.SemaphoreType.DMA((2,)),
                        pltpu.VMEM((ROWS * RES_STRIDE, 128), jnp.float32)],
        compiler_params=pltpu.CompilerParams(dimension_semantics=("arbitrary",),
                                             vmem_limit_bytes=_PEER_VMEM),
        name="peer_v",
    )(idx_rows, tab, w, h, nw)


def _peer_block(n2, n2b, h, w_query, sub_keys, u, v, norm_final_w):
    n = n2.shape[0]
    wqt = w_query.T.astype(jnp.bfloat16)
    sk = sub_keys.astype(jnp.bfloat16)
    idx_t, g_t = _peer_topk(n2b, wqt, sk)
    idx_rows = ((idx_t.T + 1) * SLAB).reshape(n // IDX_TOKENS, IDX_TOKENS * PEER_SEL)
    xt = n2.reshape(n, 2, SLAB, 128).transpose(0, 2, 1, 3).reshape(n * ROWS, 128)
    w = _peer_u(idx_rows, _pack_table(u), xt, g_t.T)
    return _peer_v(idx_rows, _pack_table(v), w, h, norm_final_w.reshape(1, D_MODEL))


HG_COLS = HG_HEADS * (2 * HG_DK + 2 * HG_DV)


def kernel(x, norm_mix_w, w_in, hgrn_lb_param, hgrn_norm_w, diff_lambda, diff_norm_w,
           w_out, norm_ffn_w, peer_w_query, peer_sub_keys, peer_u, peer_v, norm_final_w):
    bsz, seq, d = x.shape
    n = bsz * seq
    x2 = x.reshape(n, d)
    w_in_b = w_in[0].astype(jnp.bfloat16)
    proj_hg = _norm_proj(x2, norm_mix_w[0], w_in_b[:, :HG_COLS], jnp.float32).reshape(bsz, seq, HG_COLS)
    proj_da = _norm_proj(x2, norm_mix_w[0], w_in_b[:, HG_COLS:], jnp.bfloat16).reshape(bsz, seq, -1)
    hg_out = _hgrn2_pallas(proj_hg, hgrn_lb_param, hgrn_norm_w[0])
    da_out = _diff_attn_pallas(proj_da, proj_da, proj_da, diff_lambda[0], diff_norm_w[0],
                               cols=(0, DA_HEADS, 2 * DA_HEADS))
    h, n2, n2b = _out_proj(hg_out.reshape(n, -1), da_out.reshape(n, -1), w_out[0].astype(jnp.bfloat16),
                           x2, norm_ffn_w[0])
    out = _peer_block(n2, n2b, h, peer_w_query[0], peer_sub_keys[0], peer_u[0], peer_v[0], norm_final_w)
    return out.reshape(bsz, seq, d)
```

```python
import math
import jax, jax.numpy as jnp
from jax import lax
from jax.experimental import pallas as pl
from jax.experimental.pallas import tpu as pltpu

D_MODEL = 1024
CHUNK = 64
HG_HEADS = 4
HG_DK = 128
HG_DV = 128
DA_HEADS = 4
DA_DV = 128
DA_DH = 64
PEER_HEADS = 8
PEER_N_KEYS = 128
PEER_D_KEY = 256
PEER_HALF = 128
PEER_TOPK = 16
RMS_EPS = 1e-6
LANES = 128
HG_COLS = HG_HEADS * (2 * HG_DK + 2 * HG_DV)
NT_DIMS = (((1,), (1,)), ((), ()))


def _split_hi_lo(a):
    hi = a.astype(jnp.bfloat16)
    return hi, (a - hi.astype(jnp.float32)).astype(jnp.bfloat16)


PROJ_TM = 512
PROJ_VMEM = 48 * 1024 * 1024


def _in_proj_kernel(x_ref, nw_ref, w_ref, hg_ref, da_ref):
    x = x_ref[...]
    n = (x * lax.rsqrt(jnp.mean(x * x, axis=-1, keepdims=True) + RMS_EPS) * nw_ref[...]).astype(jnp.bfloat16)
    hg_ref[...] = jnp.dot(n, w_ref[:, :HG_COLS], preferred_element_type=jnp.float32)
    da_ref[...] = jnp.dot(n, w_ref[:, HG_COLS:], preferred_element_type=jnp.float32).astype(da_ref.dtype)


def _in_proj(x, nw, w):
    n, d = x.shape
    e = w.shape[1]
    tm = min(PROJ_TM, n)
    return pl.pallas_call(
        _in_proj_kernel,
        grid=(n // tm,),
        in_specs=[pl.BlockSpec((tm, d), lambda i: (i, 0)),
                  pl.BlockSpec((1, d), lambda i: (0, 0)),
                  pl.BlockSpec((d, e), lambda i: (0, 0))],
        out_specs=[pl.BlockSpec((tm, HG_COLS), lambda i: (i, 0)),
                   pl.BlockSpec((tm, e - HG_COLS), lambda i: (i, 0))],
        out_shape=[jax.ShapeDtypeStruct((n, HG_COLS), jnp.float32),
                   jax.ShapeDtypeStruct((n, e - HG_COLS), jnp.bfloat16)],
        compiler_params=pltpu.CompilerParams(dimension_semantics=("arbitrary",), vmem_limit_bytes=PROJ_VMEM),
        name="in_proj",
    )(x, nw.reshape(1, d), w)


SLAB = D_MODEL // 2 // LANES
ROWS = 2 * SLAB


def _out_proj_kernel(hg_ref, da_ref, w_ref, x_ref, nw_ref, h_ref, xt_ref, nb_ref):
    tm = x_ref.shape[0]
    half = hg_ref.shape[1]
    h = (x_ref[...]
         + jnp.dot(hg_ref[...], w_ref[pl.ds(0, half), :], preferred_element_type=jnp.float32)
         + jnp.dot(da_ref[...], w_ref[pl.ds(half, half), :], preferred_element_type=jnp.float32))
    h_ref[...] = h
    n = h * lax.rsqrt(jnp.mean(h * h, axis=-1, keepdims=True) + RMS_EPS) * nw_ref[...]
    nb_ref[...] = n.astype(jnp.bfloat16)
    for c in range(ROWS):
        r = 2 * (c % SLAB) + c // SLAB
        xt_ref[pl.ds(r, tm, stride=ROWS), :] = n[:, c * LANES:(c + 1) * LANES]


def _out_proj(hg, da, w, x, nw):
    n, d = x.shape
    half = hg.shape[1]
    tm = min(PROJ_TM, n)
    row = lambda cols: pl.BlockSpec((tm, cols), lambda i: (i, 0))
    return pl.pallas_call(
        _out_proj_kernel,
        grid=(n // tm,),
        in_specs=[row(half), row(half), pl.BlockSpec((2 * half, d), lambda i: (0, 0)), row(d),
                  pl.BlockSpec((1, d), lambda i: (0, 0))],
        out_specs=[row(d), pl.BlockSpec((tm * ROWS, LANES), lambda i: (i, 0)), row(d)],
        out_shape=[jax.ShapeDtypeStruct((n, d), jnp.float32),
                   jax.ShapeDtypeStruct((n * ROWS, LANES), jnp.float32),
                   jax.ShapeDtypeStruct((n, d), jnp.bfloat16)],
        compiler_params=pltpu.CompilerParams(dimension_semantics=("arbitrary",), vmem_limit_bytes=PROJ_VMEM),
        name="out_proj",
    )(hg, da, w, x, nw.reshape(1, d))


SUB = 16
HG_SEQ_BLOCK = 512


def _hgrn2_kernel(q_ref, f_ref, v_ref, g_ref, lb_ref, nw_ref, o_ref, state_t):
    @pl.when(pl.program_id(1) == 0)
    def _():
        state_t[...] = jnp.zeros_like(state_t)

    lp = lb_ref[...]
    le = jnp.exp(lp - jnp.max(lp, axis=0, keepdims=True))
    lb_all = le[0:1] / jnp.sum(le, axis=0, keepdims=True)
    nw = nw_ref[...]
    r64 = lax.broadcasted_iota(jnp.int32, (CHUNK, CHUNK), 0)
    c64 = lax.broadcasted_iota(jnp.int32, (CHUNK, CHUNK), 1)
    tri = (c64 <= r64).astype(jnp.bfloat16)
    row_k = lax.broadcasted_iota(jnp.int32, (CHUNK, HG_DK), 0)
    row_s = lax.broadcasted_iota(jnp.int32, (SUB, HG_DK), 0)
    lane_s = lax.broadcasted_iota(jnp.int32, (SUB, CHUNK), 1)

    def head_chunk(hd, r0):
        cols = pl.ds(hd * HG_DK, HG_DK)
        lb = lb_all[:, hd * HG_DK:(hd + 1) * HG_DK]
        f = lb + (1.0 - lb) * jax.nn.sigmoid(f_ref[pl.ds(r0, CHUNK), cols])
        k = 1.0 - f
        hi, lo = _split_hi_lo(jnp.log(f))
        b = (jnp.dot(tri, hi, preferred_element_type=jnp.float32)
             + jnp.dot(tri, lo, preferred_element_type=jnp.float32))
        q = q_ref[pl.ds(r0, CHUNK), cols] * (HG_DK ** -0.5)
        v = v_ref[pl.ds(r0, CHUNK), cols]
        strips = []
        for i in range(CHUNK // SUB):
            lo_r = i * SUB
            bi, qi, ki = b[lo_r:lo_r + SUB], q[lo_r:lo_r + SUB], k[lo_r:lo_r + SUB]
            diag = jnp.zeros((SUB, CHUNK), jnp.float32)
            for s in range(SUB):
                dec = jnp.exp(jnp.where(row_s >= s, bi - bi[s:s + 1], -jnp.inf))
                col = jnp.sum(qi * ki[s:s + 1] * dec, axis=-1, keepdims=True)
                diag = jnp.where(lane_s == lo_r + s, col, diag)
            if i == 0:
                strips.append(diag)
            else:
                ref = b[lo_r - 1:lo_r]
                qt = (qi * jnp.exp(bi - ref)).astype(jnp.bfloat16)
                kt = (k * jnp.exp(jnp.where(row_k < lo_r, ref - b, -jnp.inf))).astype(jnp.bfloat16)
                strips.append(diag + lax.dot_general(qt, kt, NT_DIMS, preferred_element_type=jnp.float32))
        a = jnp.concatenate(strips, axis=0).astype(jnp.bfloat16)
        st = state_t[hd]
        o = jnp.dot(a, v.astype(jnp.bfloat16), preferred_element_type=jnp.float32)
        o = o + lax.dot_general((q * jnp.exp(b)).astype(jnp.bfloat16), st.astype(jnp.bfloat16), NT_DIMS,
                                preferred_element_type=jnp.float32)
        b_last = b[CHUNK - 1:CHUNK]
        kd = (k * jnp.exp(b_last - b)).astype(jnp.bfloat16)
        state_t[hd] = st * jnp.exp(b_last) + jnp.dot(v.T.astype(jnp.bfloat16), kd,
                                                     preferred_element_type=jnp.float32)
        y = o * lax.rsqrt(jnp.mean(o * o, axis=-1, keepdims=True) + RMS_EPS) * nw
        o_ref[pl.ds(r0, CHUNK), cols] = (y * jax.nn.silu(g_ref[pl.ds(r0, CHUNK), cols])).astype(o_ref.dtype)

    def chunk(ci, carry):
        r0 = pl.multiple_of(ci * CHUNK, CHUNK)
        for hd in range(HG_HEADS):
            head_chunk(hd, r0)
        return carry

    lax.fori_loop(0, q_ref.shape[0] // CHUNK, chunk, 0)


def _hgrn2_pallas(proj, lb_param, norm_w):
    bsz, seq, _ = proj.shape
    sb = min(HG_SEQ_BLOCK, seq)
    nl = lb_param.shape[0]
    width = HG_HEADS * HG_DK
    col = lambda group: pl.BlockSpec((None, sb, width), lambda b, s: (b, s, group))
    return pl.pallas_call(
        _hgrn2_kernel,
        grid=(bsz, seq // sb),
        in_specs=[col(0), col(1), col(2), col(3),
                  pl.BlockSpec((nl, width), lambda b, s: (0, 0)),
                  pl.BlockSpec((1, HG_DV), lambda b, s: (0, 0))],
        out_specs=pl.BlockSpec((None, sb, width), lambda b, s: (b, s, 0)),
        out_shape=jax.ShapeDtypeStruct((bsz, seq, width), jnp.bfloat16),
        scratch_shapes=[pltpu.VMEM((HG_HEADS, HG_DV, HG_DK), jnp.float32)],
        compiler_params=pltpu.CompilerParams(dimension_semantics=("arbitrary", "arbitrary")),
        name="hgrn2",
    )(proj, proj, proj, proj, lb_param, norm_w.reshape(1, HG_DV))


DA_TILE = 512
LAMBDA_INIT = 0.8 - 0.6 * math.exp(-0.3 * 0)


def _diff_attn_kernel(q_ref, k_ref, v_ref, lp_ref, nw_ref, o_ref, m_scr, l_scr, acc_scr):
    t = q_ref.shape[0]
    qi = pl.program_id(2)
    lane = lax.broadcasted_iota(jnp.int32, (t, 2 * DA_DH), 1)
    q = q_ref[...] * jnp.asarray(DA_DH ** -0.5, jnp.bfloat16)
    qs = [jnp.where(lane < DA_DH, q, jnp.zeros_like(q)), jnp.where(lane >= DA_DH, q, jnp.zeros_like(q))]
    m_scr[...] = jnp.full(m_scr.shape, -jnp.inf, jnp.float32)
    l_scr[...] = jnp.zeros_like(l_scr)
    acc_scr[...] = jnp.zeros_like(acc_scr)

    def sweep(kj, mask):
        k0 = pl.multiple_of(kj * t, t)
        kt = k_ref[pl.ds(k0, t), :]
        vt = v_ref[pl.ds(k0, t), :]
        for c in range(2):
            s = lax.dot_general(qs[c], kt, NT_DIMS, preferred_element_type=jnp.float32)
            if mask is not None:
                s = jnp.where(mask, s, -jnp.inf)
            m_old = m_scr[c]
            m_new = jnp.maximum(m_old, jnp.max(s, axis=-1, keepdims=True))
            alpha = jnp.exp(m_old - m_new)
            p = jnp.exp(s - jnp.tile(m_new, (1, t // LANES)))
            l_scr[c] = alpha * l_scr[c] + jnp.sum(p, axis=-1, keepdims=True)
            acc_scr[c] = alpha * acc_scr[c] + jnp.dot(p.astype(jnp.bfloat16), vt,
                                                      preferred_element_type=jnp.float32)
            m_scr[c] = m_new

    def full_tile(kj, carry):
        sweep(kj, None)
        return carry

    lax.fori_loop(0, qi, full_tile, 0)
    row = lax.broadcasted_iota(jnp.int32, (t, t), 0)
    col = lax.broadcasted_iota(jnp.int32, (t, t), 1)
    sweep(qi, (col // CHUNK) <= (row // CHUNK))

    lp = lp_ref[...]
    lam = (jnp.exp(jnp.sum(lp[0:1] * lp[1:2], axis=-1, keepdims=True))
           - jnp.exp(jnp.sum(lp[2:3] * lp[3:4], axis=-1, keepdims=True)) + LAMBDA_INIT)
    o = acc_scr[0] / l_scr[0] - lam * (acc_scr[1] / l_scr[1])
    y = o * lax.rsqrt(jnp.mean(o * o, axis=-1, keepdims=True) + RMS_EPS) * nw_ref[...]
    o_ref[...] = (y * (1.0 - LAMBDA_INIT)).astype(o_ref.dtype)


def _diff_attn_pallas(q, k, v, lam_params, norm_w, cols=(0, 0, 0)):
    bsz, seq, _ = q.shape
    t = min(DA_TILE, seq)
    cq, ck, cv = cols
    return pl.pallas_call(
        _diff_attn_kernel,
        grid=(bsz, DA_HEADS, seq // t),
        in_specs=[pl.BlockSpec((None, t, 2 * DA_DH), lambda b, h, i: (b, i, cq + h)),
                  pl.BlockSpec((None, seq, 2 * DA_DH), lambda b, h, i: (b, 0, ck + h)),
                  pl.BlockSpec((None, seq, DA_DV), lambda b, h, i: (b, 0, cv + h)),
                  pl.BlockSpec((4, DA_DH), lambda b, h, i: (0, 0)),
                  pl.BlockSpec((1, DA_DV), lambda b, h, i: (0, 0))],
        out_specs=pl.BlockSpec((None, t, DA_DV), lambda b, h, i: (b, i, h)),
        out_shape=jax.ShapeDtypeStruct((bsz, seq, DA_HEADS * DA_DV), jnp.bfloat16),
        scratch_shapes=[pltpu.VMEM((2, t, LANES), jnp.float32), pltpu.VMEM((2, t, LANES), jnp.float32),
                        pltpu.VMEM((2, t, DA_DV), jnp.float32)],
        compiler_params=pltpu.CompilerParams(dimension_semantics=("arbitrary", "arbitrary", "arbitrary")),
        name="diff_attn",
    )(q, k, v, lam_params, norm_w.reshape(1, DA_DV))


PEER_SEL = PEER_HEADS * PEER_TOPK
GATHER_COLS = PEER_SEL * ROWS
_CAND_NB = [PEER_TOPK // (a + 1) for a in range(PEER_TOPK)]
_CAND_ROWS = 56
TOPK_TB = 256


def _topk_rows(s, k):
    n = s.shape[0]
    iota = lax.broadcasted_iota(jnp.int32, s.shape, 0)
    vals, idxs = [], []
    for _ in range(k):
        m = jnp.max(s, axis=0, keepdims=True)
        idx = jnp.min(jnp.where(s == m, iota, n), axis=0, keepdims=True)
        s = jnp.where(iota == idx, -jnp.inf, s)
        vals.append(m)
        idxs.append(idx)
    return jnp.concatenate(vals, axis=0), jnp.concatenate(idxs, axis=0)


def _peer_topk_kernel(n2_ref, wqt_ref, sk_ref, idx_ref, g_ref, qh_scr, eid_scr, gate_scr):
    qh_scr[...] = lax.dot_general(wqt_ref[...], n2_ref[...], NT_DIMS, preferred_element_type=jnp.float32)

    def head(h):
        base = pl.multiple_of(h * PEER_D_KEY, PEER_D_KEY)
        q1 = qh_scr[pl.ds(base, PEER_HALF), :].astype(jnp.bfloat16)
        q2 = qh_scr[pl.ds(base + PEER_HALF, PEER_HALF), :].astype(jnp.bfloat16)
        s1 = jnp.dot(sk_ref[h, 0], q1, preferred_element_type=jnp.float32)
        s2 = jnp.dot(sk_ref[h, 1], q2, preferred_element_type=jnp.float32)
        v1, i1 = _topk_rows(s1, PEER_TOPK)
        v2, i2 = _topk_rows(s2, PEER_TOPK)
        tb = v1.shape[1]
        cand = jnp.concatenate(
            [v1[a:a + 1] + v2[:_CAND_NB[a]] for a in range(PEER_TOPK)]
            + [jnp.full((_CAND_ROWS - sum(_CAND_NB), tb), -jnp.inf, jnp.float32)], axis=0)
        cidx = jnp.concatenate(
            [i1[a:a + 1] * PEER_N_KEYS + i2[:_CAND_NB[a]] for a in range(PEER_TOPK)]
            + [jnp.zeros((_CAND_ROWS - sum(_CAND_NB), tb), jnp.int32)], axis=0)
        ts, pos = _topk_rows(cand, PEER_TOPK)
        iota = lax.broadcasted_iota(jnp.int32, cand.shape, 0)
        eid = jnp.concatenate(
            [jnp.sum(jnp.where(iota == pos[r:r + 1], cidx, 0), axis=0, keepdims=True)
             for r in range(PEER_TOPK)], axis=0)
        e = jnp.exp(ts - ts[0:1])
        g = e / jnp.sum(e, axis=0, keepdims=True)
        row = pl.multiple_of(h * PEER_TOPK, PEER_TOPK)
        eid_scr[pl.ds(row, PEER_TOPK), :] = eid
        gate_scr[pl.ds(row, PEER_TOPK), :] = g

    def head_pair(i, carry):
        head(2 * i)
        head(2 * i + 1)
        return carry

    lax.fori_loop(0, PEER_HEADS // 2, head_pair, 0)
    idx_ref[...] = ((eid_scr[...] + 1) * SLAB).T
    g_ref[...] = gate_scr[...].T


def _peer_topk(n2b, wqt, sk):
    n = n2b.shape[0]
    tb = min(TOPK_TB, n)
    return pl.pallas_call(
        _peer_topk_kernel,
        grid=(n // tb,),
        in_specs=[pl.BlockSpec((tb, D_MODEL), lambda i: (i, 0)),
                  pl.BlockSpec(wqt.shape, lambda i: (0, 0)),
                  pl.BlockSpec(sk.shape, lambda i: (0, 0, 0, 0))],
        out_specs=[pl.BlockSpec((tb, PEER_SEL), lambda i: (i, 0)),
                   pl.BlockSpec((tb, PEER_SEL), lambda i: (i, 0))],
        out_shape=[jax.ShapeDtypeStruct((n, PEER_SEL), jnp.int32),
                   jax.ShapeDtypeStruct((n, PEER_SEL), jnp.float32)],
        scratch_shapes=[pltpu.VMEM((PEER_HEADS * PEER_D_KEY, tb), jnp.float32),
                        pltpu.VMEM((PEER_SEL, tb), jnp.int32),
                        pltpu.VMEM((PEER_SEL, tb), jnp.float32)],
        compiler_params=pltpu.CompilerParams(dimension_semantics=("arbitrary",),
                                             vmem_limit_bytes=40 * 1024 * 1024),
        name="peer_topk",
    )(n2b, wqt, sk)


def _pack_table(t):
    e = t.shape[0]
    tb = t.astype(jnp.bfloat16).reshape(e, 2, SLAB, LANES)
    bits = lax.bitcast_convert_type(tb, jnp.uint16).astype(jnp.uint32)
    word = jnp.pad(bits[:, 0] | (bits[:, 1] << 16), ((1, 1), (0, 0), (0, 0)))
    return lax.bitcast_convert_type(word, jnp.int32).reshape((e + 2) * SLAB, LANES)


IDX_TOKENS = 32
GROUP = 8
PEER_VMEM = 56 * 1024 * 1024


def _index_copy(idx_hbm, ibuf, sem, blk, slot):
    return pltpu.make_async_copy(idx_hbm.at[blk], ibuf.at[slot], sem.at[slot])


def _for_both_index_buffers(idx_hbm, ibuf, sem, process):
    i = pl.program_id(0)

    @pl.when(i == 0)
    def _():
        _index_copy(idx_hbm, ibuf, sem, 0, 0).start()

    _index_copy(idx_hbm, ibuf, sem, 2 * i + 1, 1).start()
    _index_copy(idx_hbm, ibuf, sem, 2 * i, 0).wait()
    process(0)

    @pl.when(i + 1 < pl.num_programs(0))
    def _():
        _index_copy(idx_hbm, ibuf, sem, 2 * i + 2, 0).start()

    _index_copy(idx_hbm, ibuf, sem, 2 * i + 1, 1).wait()
    process(1)


def _gather_matrix(ibuf, slot, base, tab_ref):
    sub = lax.broadcasted_iota(jnp.int32, (2 * SLAB, LANES), 0)
    pairs = []
    for p in range(PEER_SEL // 2):
        ia = pl.multiple_of(ibuf[slot, base + 2 * p], SLAB)
        ib = pl.multiple_of(ibuf[slot, base + 2 * p + 1], SLAB)
        a = tab_ref[pl.ds(ia, 2 * SLAB), :]
        b = tab_ref[pl.ds(ib - SLAB, 2 * SLAB), :]
        pairs.append(jnp.where(sub < SLAB, a, b))
    return pltpu.bitcast(jnp.concatenate(pairs, axis=0), jnp.bfloat16)


def _own_row_mask():
    r = lax.broadcasted_iota(jnp.int32, (ROWS, GATHER_COLS), 0)
    c = lax.broadcasted_iota(jnp.int32, (ROWS, GATHER_COLS), 1)
    return (c % ROWS) == r


def _peer_u_kernel(idx_hbm, tab_ref, xt_ref, g_ref, w_ref, ibuf, sem, p_scr):
    own = _own_row_mask()

    def process(slot):
        for gi in range(IDX_TOKENS // GROUP):
            r0 = slot * IDX_TOKENS + gi * GROUP
            parts = []
            for k in range(GROUP):
                t = _gather_matrix(ibuf, slot, (gi * GROUP + k) * PEER_SEL, tab_ref)
                x8 = xt_ref[pl.ds(ROWS * (r0 + k), ROWS), :].astype(jnp.bfloat16)
                r = lax.dot_general(x8, t, NT_DIMS, preferred_element_type=jnp.float32)
                parts.append(jnp.sum(jnp.where(own, r, 0.0), axis=0, keepdims=True))
            p_scr[pl.ds(r0, GROUP), :] = jnp.concatenate(parts, axis=0)

    _for_both_index_buffers(idx_hbm, ibuf, sem, process)

    col = lax.broadcasted_iota(jnp.int32, (GATHER_COLS, PEER_SEL), 0)
    exp = lax.broadcasted_iota(jnp.int32, (GATHER_COLS, PEER_SEL), 1)
    fold = ((col // ROWS) == exp).astype(jnp.bfloat16)
    hi, lo = _split_hi_lo(p_scr[...])
    act = (jnp.dot(hi, fold, preferred_element_type=jnp.float32)
           + jnp.dot(lo, fold, preferred_element_type=jnp.float32))
    hi, lo = _split_hi_lo(g_ref[...] * jax.nn.gelu(act))
    w_ref[...] = (lax.dot_general(hi, fold, NT_DIMS, preferred_element_type=jnp.float32)
                  + lax.dot_general(lo, fold, NT_DIMS, preferred_element_type=jnp.float32))


RES_STRIDE = 72


def _peer_v_kernel(idx_hbm, tab_ref, w_ref, h_ref, nw_ref, o_ref, ibuf, sem, res):
    own = _own_row_mask()
    tb = h_ref.shape[0]

    def process(slot):
        for gi in range(IDX_TOKENS // GROUP):
            r0 = slot * IDX_TOKENS + gi * GROUP
            wg = w_ref[pl.ds(r0, GROUP), :]
            for k in range(GROUP):
                t = _gather_matrix(ibuf, slot, (gi * GROUP + k) * PEER_SEL, tab_ref)
                w8 = jnp.where(own, wg[k:k + 1], 0.0).astype(jnp.bfloat16)
                res[pl.ds(r0 + k, ROWS, stride=RES_STRIDE), :] = jnp.dot(w8, t, preferred_element_type=jnp.float32)

    _for_both_index_buffers(idx_hbm, ibuf, sem, process)

    planes = [res[pl.ds((2 * (c % SLAB) + c // SLAB) * RES_STRIDE, tb), :] for c in range(ROWS)]
    y = h_ref[...] + jnp.concatenate(planes, axis=-1)
    y = y * lax.rsqrt(jnp.mean(y * y, axis=-1, keepdims=True) + RMS_EPS)
    o_ref[...] = y * nw_ref[...]


def _peer_u(idx_rows, tab, xt, g):
    tb = 2 * IDX_TOKENS
    n = g.shape[0]
    return pl.pallas_call(
        _peer_u_kernel,
        grid=(n // tb,),
        in_specs=[pl.BlockSpec(memory_space=pl.ANY),
                  pl.BlockSpec(memory_space=pltpu.VMEM),
                  pl.BlockSpec((tb * ROWS, LANES), lambda i: (i, 0)),
                  pl.BlockSpec((tb, PEER_SEL), lambda i: (i, 0))],
        out_specs=pl.BlockSpec((tb, GATHER_COLS), lambda i: (i, 0)),
        out_shape=jax.ShapeDtypeStruct((n, GATHER_COLS), jnp.float32),
        scratch_shapes=[pltpu.SMEM((2, IDX_TOKENS * PEER_SEL), jnp.int32),
                        pltpu.SemaphoreType.DMA((2,)),
                        pltpu.VMEM((tb, GATHER_COLS), jnp.float32)],
        compiler_params=pltpu.CompilerParams(dimension_semantics=("arbitrary",), vmem_limit_bytes=PEER_VMEM),
        name="peer_u",
    )(idx_rows, tab, xt, g)


def _peer_v(idx_rows, tab, w, h, nw):
    tb = 2 * IDX_TOKENS
    n = h.shape[0]
    return pl.pallas_call(
        _peer_v_kernel,
        grid=(n // tb,),
        in_specs=[pl.BlockSpec(memory_space=pl.ANY),
                  pl.BlockSpec(memory_space=pltpu.VMEM),
                  pl.BlockSpec((tb, GATHER_COLS), lambda i: (i, 0)),
                  pl.BlockSpec((tb, D_MODEL), lambda i: (i, 0)),
                  pl.BlockSpec((1, D_MODEL), lambda i: (0, 0))],
        out_specs=pl.BlockSpec((tb, D_MODEL), lambda i: (i, 0)),
        out_shape=jax.ShapeDtypeStruct((n, D_MODEL), jnp.float32),
        scratch_shapes=[pltpu.SMEM((2, IDX_TOKENS * PEER_SEL), jnp.int32),
                        pltpu.SemaphoreType.DMA((2,)),
                        pltpu.VMEM((ROWS * RES_STRIDE, LANES), jnp.float32)],
        compiler_params=pltpu.CompilerParams(dimension_semantics=("arbitrary",), vmem_limit_bytes=PEER_VMEM),
        name="peer_v",
    )(idx_rows, tab, w, h, nw)


def _peer_block(xt, n2b, h, w_query, sub_keys, u, v, norm_final_w):
    n = h.shape[0]
    idx, g = _peer_topk(n2b, w_query.T.astype(jnp.bfloat16), sub_keys.astype(jnp.bfloat16))
    idx_rows = idx.reshape(n // IDX_TOKENS, IDX_TOKENS * PEER_SEL)
    w = _peer_u(idx_rows, _pack_table(u), xt, g)
    return _peer_v(idx_rows, _pack_table(v), w, h, norm_final_w.reshape(1, D_MODEL))


def kernel(x, norm_mix_w, w_in, hgrn_lb_param, hgrn_norm_w, diff_lambda, diff_norm_w,
           w_out, norm_ffn_w, peer_w_query, peer_sub_keys, peer_u, peer_v, norm_final_w):
    bsz, seq, d = x.shape
    n = bsz * seq
    x2 = x.reshape(n, d)
    proj_hg, proj_da = _in_proj(x2, norm_mix_w[0], w_in[0].astype(jnp.bfloat16))
    hg_out = _hgrn2_pallas(proj_hg.reshape(bsz, seq, -1), hgrn_lb_param, hgrn_norm_w[0])
    proj_da = proj_da.reshape(bsz, seq, -1)
    da_out = _diff_attn_pallas(proj_da, proj_da, proj_da, diff_lambda[0], diff_norm_w[0],
                               cols=(0, DA_HEADS, 2 * DA_HEADS))
    h, xt, n2b = _out_proj(hg_out.reshape(n, -1), da_out.reshape(n, -1), w_out[0].astype(jnp.bfloat16),
                           x2, norm_ffn_w[0])
    out = _peer_block(xt, n2b, h, peer_w_query[0], peer_sub_keys[0], peer_u[0], peer_v[0], norm_final_w)
    return out.reshape(bsz, seq, d)
```

```python
import math
import jax, jax.numpy as jnp
from jax import lax
from jax.experimental import pallas as pl
from jax.experimental.pallas import tpu as pltpu

D_MODEL = 1024
CHUNK = 64
HG_HEADS = 4
HG_DK = 128
HG_DV = 128
DA_HEADS = 4
DA_DV = 128
DA_DH = 64
PEER_HEADS = 8
PEER_N_KEYS = 128
PEER_D_KEY = 256
PEER_HALF = 128
PEER_TOPK = 16
RMS_EPS = 1e-6
LANES = 128
HG_COLS = HG_HEADS * (2 * HG_DK + 2 * HG_DV)
NT_DIMS = (((1,), (1,)), ((), ()))


def _split_hi_lo(a):
    hi = a.astype(jnp.bfloat16)
    return hi, (a - hi.astype(jnp.float32)).astype(jnp.bfloat16)


PROJ_TM = 512
PROJ_VMEM = 48 * 1024 * 1024


def _in_proj_kernel(x_ref, nw_ref, w_ref, hg_ref, da_ref):
    x = x_ref[...]
    n = (x * lax.rsqrt(jnp.mean(x * x, axis=-1, keepdims=True) + RMS_EPS) * nw_ref[...]).astype(jnp.bfloat16)
    hg_ref[...] = jnp.dot(n, w_ref[:, :HG_COLS], preferred_element_type=jnp.float32)
    da_ref[...] = jnp.dot(n, w_ref[:, HG_COLS:], preferred_element_type=jnp.float32).astype(da_ref.dtype)


def _in_proj(x, nw, w):
    n, d = x.shape
    e = w.shape[1]
    tm = min(PROJ_TM, n)
    return pl.pallas_call(
        _in_proj_kernel,
        grid=(n // tm,),
        in_specs=[pl.BlockSpec((tm, d), lambda i: (i, 0)),
                  pl.BlockSpec((1, d), lambda i: (0, 0)),
                  pl.BlockSpec((d, e), lambda i: (0, 0))],
        out_specs=[pl.BlockSpec((tm, HG_COLS), lambda i: (i, 0)),
                   pl.BlockSpec((tm, e - HG_COLS), lambda i: (i, 0))],
        out_shape=[jax.ShapeDtypeStruct((n, HG_COLS), jnp.float32),
                   jax.ShapeDtypeStruct((n, e - HG_COLS), jnp.bfloat16)],
        compiler_params=pltpu.CompilerParams(dimension_semantics=("arbitrary",), vmem_limit_bytes=PROJ_VMEM),
        name="in_proj",
    )(x, nw.reshape(1, d), w)


SLAB = D_MODEL // 2 // LANES
ROWS = 2 * SLAB


def _out_proj_kernel(hg_ref, da_ref, w_ref, x_ref, nw_ref, h_ref, xt_ref, nb_ref):
    tm = x_ref.shape[0]
    half = hg_ref.shape[1]
    h = (x_ref[...]
         + jnp.dot(hg_ref[...], w_ref[pl.ds(0, half), :], preferred_element_type=jnp.float32)
         + jnp.dot(da_ref[...], w_ref[pl.ds(half, half), :], preferred_element_type=jnp.float32))
    h_ref[...] = h
    n = h * lax.rsqrt(jnp.mean(h * h, axis=-1, keepdims=True) + RMS_EPS) * nw_ref[...]
    nb_ref[...] = n.astype(jnp.bfloat16)
    for c in range(ROWS):
        r = 2 * (c % SLAB) + c // SLAB
        xt_ref[pl.ds(r, tm, stride=ROWS), :] = n[:, c * LANES:(c + 1) * LANES]


def _out_proj(hg, da, w, x, nw):
    n, d = x.shape
    half = hg.shape[1]
    tm = min(PROJ_TM, n)
    row = lambda cols: pl.BlockSpec((tm, cols), lambda i: (i, 0))
    return pl.pallas_call(
        _out_proj_kernel,
        grid=(n // tm,),
        in_specs=[row(half), row(half), pl.BlockSpec((2 * half, d), lambda i: (0, 0)), row(d),
                  pl.BlockSpec((1, d), lambda i: (0, 0))],
        out_specs=[row(d), pl.BlockSpec((tm * ROWS, LANES), lambda i: (i, 0)), row(d)],
        out_shape=[jax.ShapeDtypeStruct((n, d), jnp.float32),
                   jax.ShapeDtypeStruct((n * ROWS, LANES), jnp.float32),
                   jax.ShapeDtypeStruct((n, d), jnp.bfloat16)],
        compiler_params=pltpu.CompilerParams(dimension_semantics=("arbitrary",), vmem_limit_bytes=PROJ_VMEM),
        name="out_proj",
    )(hg, da, w, x, nw.reshape(1, d))


SUB = 16
HG_SEQ_BLOCK = 512


def _hgrn2_kernel(q_ref, f_ref, v_ref, g_ref, lb_ref, nw_ref, o_ref, state_t):
    @pl.when(pl.program_id(1) == 0)
    def _():
        state_t[...] = jnp.zeros_like(state_t)

    lp = lb_ref[...]
    le = jnp.exp(lp - jnp.max(lp, axis=0, keepdims=True))
    lb_all = le[0:1] / jnp.sum(le, axis=0, keepdims=True)
    nw = nw_ref[...]
    r64 = lax.broadcasted_iota(jnp.int32, (CHUNK, CHUNK), 0)
    c64 = lax.broadcasted_iota(jnp.int32, (CHUNK, CHUNK), 1)
    tri = (c64 <= r64).astype(jnp.bfloat16)
    row_k = lax.broadcasted_iota(jnp.int32, (CHUNK, HG_DK), 0)
    row_s = lax.broadcasted_iota(jnp.int32, (SUB, HG_DK), 0)
    lane_s = lax.broadcasted_iota(jnp.int32, (SUB, CHUNK), 1)

    def head_chunk(hd, r0):
        cols = pl.ds(hd * HG_DK, HG_DK)
        lb = lb_all[:, hd * HG_DK:(hd + 1) * HG_DK]
        f = lb + (1.0 - lb) * jax.nn.sigmoid(f_ref[pl.ds(r0, CHUNK), cols])
        k = 1.0 - f
        hi, lo = _split_hi_lo(jnp.log(f))
        b = (jnp.dot(tri, hi, preferred_element_type=jnp.float32)
             + jnp.dot(tri, lo, preferred_element_type=jnp.float32))
        q = q_ref[pl.ds(r0, CHUNK), cols] * (HG_DK ** -0.5)
        v = v_ref[pl.ds(r0, CHUNK), cols]
        strips = []
        for i in range(CHUNK // SUB):
            lo_r = i * SUB
            bi, qi, ki = b[lo_r:lo_r + SUB], q[lo_r:lo_r + SUB], k[lo_r:lo_r + SUB]
            diag = jnp.zeros((SUB, CHUNK), jnp.float32)
            for s in range(SUB):
                dec = jnp.exp(jnp.where(row_s >= s, bi - bi[s:s + 1], -jnp.inf))
                col = jnp.sum(qi * ki[s:s + 1] * dec, axis=-1, keepdims=True)
                diag = jnp.where(lane_s == lo_r + s, col, diag)
            if i == 0:
                strips.append(diag)
            else:
                ref = b[lo_r - 1:lo_r]
                qt = (qi * jnp.exp(bi - ref)).astype(jnp.bfloat16)
                kt = (k * jnp.exp(jnp.where(row_k < lo_r, ref - b, -jnp.inf))).astype(jnp.bfloat16)
                strips.append(diag + lax.dot_general(qt, kt, NT_DIMS, preferred_element_type=jnp.float32))
        a = jnp.concatenate(strips, axis=0).astype(jnp.bfloat16)
        st = state_t[hd]
        o = jnp.dot(a, v.astype(jnp.bfloat16), preferred_element_type=jnp.float32)
        o = o + lax.dot_general((q * jnp.exp(b)).astype(jnp.bfloat16), st.astype(jnp.bfloat16), NT_DIMS,
                                preferred_element_type=jnp.float32)
        b_last = b[CHUNK - 1:CHUNK]
        kd = (k * jnp.exp(b_last - b)).astype(jnp.bfloat16)
        state_t[hd] = st * jnp.exp(b_last) + jnp.dot(v.T.astype(jnp.bfloat16), kd,
                                                     preferred_element_type=jnp.float32)
        y = o * lax.rsqrt(jnp.mean(o * o, axis=-1, keepdims=True) + RMS_EPS) * nw
        o_ref[pl.ds(r0, CHUNK), cols] = (y * jax.nn.silu(g_ref[pl.ds(r0, CHUNK), cols])).astype(o_ref.dtype)

    def chunk(ci, carry):
        r0 = pl.multiple_of(ci * CHUNK, CHUNK)
        for hd in range(HG_HEADS):
            head_chunk(hd, r0)
        return carry

    lax.fori_loop(0, q_ref.shape[0] // CHUNK, chunk, 0)


def _hgrn2_pallas(proj, lb_param, norm_w):
    bsz, seq, _ = proj.shape
    sb = min(HG_SEQ_BLOCK, seq)
    nl = lb_param.shape[0]
    width = HG_HEADS * HG_DK
    col = lambda group: pl.BlockSpec((None, sb, width), lambda b, s: (b, s, group))
    return pl.pallas_call(
        _hgrn2_kernel,
        grid=(bsz, seq // sb),
        in_specs=[col(0), col(1), col(2), col(3),
                  pl.BlockSpec((nl, width), lambda b, s: (0, 0)),
                  pl.BlockSpec((1, HG_DV), lambda b, s: (0, 0))],
        out_specs=pl.BlockSpec((None, sb, width), lambda b, s: (b, s, 0)),
        out_shape=jax.ShapeDtypeStruct((bsz, seq, width), jnp.bfloat16),
        scratch_shapes=[pltpu.VMEM((HG_HEADS, HG_DV, HG_DK), jnp.float32)],
        compiler_params=pltpu.CompilerParams(dimension_semantics=("arbitrary", "arbitrary")),
        name="hgrn2",
    )(proj, proj, proj, proj, lb_param, norm_w.reshape(1, HG_DV))


DA_TILE = 512
LAMBDA_INIT = 0.8 - 0.6 * math.exp(-0.3 * 0)


def _diff_attn_kernel(q_ref, k_ref, v_ref, lp_ref, nw_ref, o_ref, m_scr, l_scr, acc_scr):
    t = q_ref.shape[0]
    qi = pl.program_id(2)
    lane = lax.broadcasted_iota(jnp.int32, (t, 2 * DA_DH), 1)
    q = q_ref[...] * jnp.asarray(DA_DH ** -0.5, jnp.bfloat16)
    qs = [jnp.where(lane < DA_DH, q, jnp.zeros_like(q)), jnp.where(lane >= DA_DH, q, jnp.zeros_like(q))]
    m_scr[...] = jnp.full(m_scr.shape, -jnp.inf, jnp.float32)
    l_scr[...] = jnp.zeros_like(l_scr)
    acc_scr[...] = jnp.zeros_like(acc_scr)

    def sweep(kj, mask):
        k0 = pl.multiple_of(kj * t, t)
        kt = k_ref[pl.ds(k0, t), :]
        vt = v_ref[pl.ds(k0, t), :]
        for c in range(2):
            s = lax.dot_general(qs[c], kt, NT_DIMS, preferred_element_type=jnp.float32)
            if mask is not None:
                s = jnp.where(mask, s, -jnp.inf)
            m_old = m_scr[c]
            m_new = jnp.maximum(m_old, jnp.max(s, axis=-1, keepdims=True))
            alpha = jnp.exp(m_old - m_new)
            p = jnp.exp(s - jnp.tile(m_new, (1, t // LANES)))
            l_scr[c] = alpha * l_scr[c] + jnp.sum(p, axis=-1, keepdims=True)
            acc_scr[c] = alpha * acc_scr[c] + jnp.dot(p.astype(jnp.bfloat16), vt,
                                                      preferred_element_type=jnp.float32)
            m_scr[c] = m_new

    def full_tile(kj, carry):
        sweep(kj, None)
        return carry

    lax.fori_loop(0, qi, full_tile, 0)
    row = lax.broadcasted_iota(jnp.int32, (t, t), 0)
    col = lax.broadcasted_iota(jnp.int32, (t, t), 1)
    sweep(qi, (col // CHUNK) <= (row // CHUNK))

    lp = lp_ref[...]
    lam = (jnp.exp(jnp.sum(lp[0:1] * lp[1:2], axis=-1, keepdims=True))
           - jnp.exp(jnp.sum(lp[2:3] * lp[3:4], axis=-1, keepdims=True)) + LAMBDA_INIT)
    o = acc_scr[0] / l_scr[0] - lam * (acc_scr[1] / l_scr[1])
    y = o * lax.rsqrt(jnp.mean(o * o, axis=-1, keepdims=True) + RMS_EPS) * nw_ref[...]
    o_ref[...] = (y * (1.0 - LAMBDA_INIT)).astype(o_ref.dtype)


def _diff_attn_pallas(q, k, v, lam_params, norm_w, cols=(0, 0, 0)):
    bsz, seq, _ = q.shape
    t = min(DA_TILE, seq)
    cq, ck, cv = cols
    return pl.pallas_call(
        _diff_attn_kernel,
        grid=(bsz, DA_HEADS, seq // t),
        in_specs=[pl.BlockSpec((None, t, 2 * DA_DH), lambda b, h, i: (b, i, cq + h)),
                  pl.BlockSpec((None, seq, 2 * DA_DH), lambda b, h, i: (b, 0, ck + h)),
                  pl.BlockSpec((None, seq, DA_DV), lambda b, h, i: (b, 0, cv + h)),
                  pl.BlockSpec((4, DA_DH), lambda b, h, i: (0, 0)),
                  pl.BlockSpec((1, DA_DV), lambda b, h, i: (0, 0))],
        out_specs=pl.BlockSpec((None, t, DA_DV), lambda b, h, i: (b, i, h)),
        out_shape=jax.ShapeDtypeStruct((bsz, seq, DA_HEADS * DA_DV), jnp.bfloat16),
        scratch_shapes=[pltpu.VMEM((2, t, LANES), jnp.float32), pltpu.VMEM((2, t, LANES), jnp.float32),
                        pltpu.VMEM((2, t, DA_DV), jnp.float32)],
        compiler_params=pltpu.CompilerParams(dimension_semantics=("arbitrary", "arbitrary", "arbitrary")),
        name="diff_attn",
    )(q, k, v, lam_params, norm_w.reshape(1, DA_DV))


PEER_SEL = PEER_HEADS * PEER_TOPK
GATHER_COLS = PEER_SEL * ROWS
_CAND_NB = [PEER_TOPK // (a + 1) for a in range(PEER_TOPK)]
_CAND_ROWS = 56
TOPK_TB = 8 * LANES


def _topk_rows(s, k):
    n = s.shape[0]
    iota = lax.broadcasted_iota(jnp.int32, s.shape, 0)
    vals, idxs = [], []
    for _ in range(k):
        m = jnp.max(s, axis=0, keepdims=True)
        idx = jnp.min(jnp.where(s == m, iota, n), axis=0, keepdims=True)
        s = jnp.where(iota == idx, -jnp.inf, s)
        vals.append(m)
        idxs.append(idx)
    return jnp.concatenate(vals, axis=0), jnp.concatenate(idxs, axis=0)


def _select_exact(s1, s2):
    v1, i1 = _topk_rows(s1, PEER_TOPK)
    v2, i2 = _topk_rows(s2, PEER_TOPK)
    tb = v1.shape[1]
    cand = jnp.concatenate(
        [v1[a:a + 1] + v2[:_CAND_NB[a]] for a in range(PEER_TOPK)]
        + [jnp.full((_CAND_ROWS - sum(_CAND_NB), tb), -jnp.inf, jnp.float32)], axis=0)
    cidx = jnp.concatenate(
        [i1[a:a + 1] * PEER_N_KEYS + i2[:_CAND_NB[a]] for a in range(PEER_TOPK)]
        + [jnp.zeros((_CAND_ROWS - sum(_CAND_NB), tb), jnp.int32)], axis=0)
    ts, pos = _topk_rows(cand, PEER_TOPK)
    iota = lax.broadcasted_iota(jnp.int32, cand.shape, 0)
    eid = jnp.concatenate(
        [jnp.sum(jnp.where(iota == pos[r:r + 1], cidx, 0), axis=0, keepdims=True)
         for r in range(PEER_TOPK)], axis=0)
    e = jnp.exp(ts - ts[0:1])
    return eid, e / jnp.sum(e, axis=0, keepdims=True)


def _sort16_pairs():
    n, pairs, p = 16, [], 1
    while p < n:
        k = p
        while k >= 1:
            for j in range(k % p, n - k, 2 * k):
                for i in range(min(k, n - j - k)):
                    if (i + j) // (2 * p) == (i + j + k) // (2 * p):
                        pairs.append((i + j, i + j + k))
            k //= 2
        p *= 2
    return pairs


_SORT16 = _sort16_pairs()


def _exchange(a, b):
    (va, ia), (vb, ib) = a, b
    swap = vb > va
    return (jnp.maximum(va, vb), jnp.where(swap, ib, ia)), (jnp.minimum(va, vb), jnp.where(swap, ia, ib))


def _top16_sorted(items):
    groups = []
    for g0 in range(0, len(items), 16):
        grp = list(items[g0:g0 + 16])
        for i, j in _SORT16:
            grp[i], grp[j] = _exchange(grp[i], grp[j])
        groups.append(grp)
    dropped = None
    while len(groups) > 1:
        merged = []
        for a, b in zip(groups[0::2], groups[1::2]):
            top = []
            for i in range(16):
                hi, lo = _exchange(a[i], b[15 - i])
                top.append(hi)
                dropped = lo[0] if dropped is None else jnp.maximum(dropped, lo[0])
            d = 8
            while d >= 1:
                for i in range(16):
                    if (i & d) == 0:
                        top[i], top[i + d] = _exchange(top[i], top[i + d])
                d //= 2
            merged.append(top)
        groups = merged
    return groups[0], dropped


def _has_tie(top, dropped):
    tie = top[PEER_TOPK - 1][0] == dropped
    for i in range(PEER_TOPK - 1):
        tie = tie | (top[i][0] == top[i + 1][0])
    return tie


def _peer_topk_kernel(n2_ref, wqt_ref, sk_ref, idx_ref, g_ref, qh_scr, s1_scr, s2_scr, k1_scr, k2_scr,
                      eid_scr, gate_scr):
    qh_scr[...] = lax.dot_general(wqt_ref[...], n2_ref[...], NT_DIMS, preferred_element_type=jnp.float32)
    groups = n2_ref.shape[0] // LANES

    def key_tiles(k_scr, n):
        return [(k_scr[pl.ds(8 * k, 8), :], jnp.full((8, LANES), k, jnp.int32)) for k in range(n)]

    def head(h, carry):
        base = pl.multiple_of(h * PEER_D_KEY, PEER_D_KEY)
        q1 = qh_scr[pl.ds(base, PEER_HALF), :].astype(jnp.bfloat16)
        q2 = qh_scr[pl.ds(base + PEER_HALF, PEER_HALF), :].astype(jnp.bfloat16)
        s1_scr[...] = jnp.dot(sk_ref[h, 0], q1, preferred_element_type=jnp.float32)
        s2_scr[...] = jnp.dot(sk_ref[h, 1], q2, preferred_element_type=jnp.float32)
        for j in range(groups):
            k1_scr[pl.ds(j, PEER_N_KEYS, stride=8), :] = s1_scr[:, j * LANES:(j + 1) * LANES]
            k2_scr[pl.ds(j, PEER_N_KEYS, stride=8), :] = s2_scr[:, j * LANES:(j + 1) * LANES]

        top1, drop1 = _top16_sorted(key_tiles(k1_scr, PEER_N_KEYS))
        top2, drop2 = _top16_sorted(key_tiles(k2_scr, PEER_N_KEYS))
        cands = [(top1[a][0] + top2[b][0], top1[a][1] * PEER_N_KEYS + top2[b][1])
                 for a in range(PEER_TOPK) for b in range(_CAND_NB[a])]
        pad = (jnp.full((8, LANES), -jnp.inf, jnp.float32), jnp.zeros((8, LANES), jnp.int32))
        cands += [pad] * (64 - len(cands))
        top, dropc = _top16_sorted(cands)
        es = [jnp.exp(t[0] - top[0][0]) for t in top]
        den = es[0]
        for e in es[1:]:
            den = den + e
        slot0 = h * PEER_TOPK
        for r in range(PEER_TOPK):
            row = pl.multiple_of((slot0 + r) * 8, 8)
            eid_scr[pl.ds(row, 8), :] = top[r][1]
            gate_scr[pl.ds(row, 8), :] = es[r] / den

        tie = _has_tie(top1, drop1) | _has_tie(top2, drop2) | _has_tie(top, dropc)

        @pl.when(jnp.any(tie))
        def _():
            eid, gate = _select_exact(s1_scr[...], s2_scr[...])
            for r in range(PEER_TOPK):
                row = pl.multiple_of((slot0 + r) * 8, 8)
                eid_scr[pl.ds(row, 8), :] = jnp.concatenate(
                    [eid[r:r + 1, j * LANES:(j + 1) * LANES] for j in range(groups)], axis=0)
                gate_scr[pl.ds(row, 8), :] = jnp.concatenate(
                    [gate[r:r + 1, j * LANES:(j + 1) * LANES] for j in range(groups)], axis=0)

        return carry

    lax.fori_loop(0, PEER_HEADS, head, 0)
    for j in range(groups):
        eid = eid_scr[pl.ds(j, PEER_SEL, stride=8), :]
        idx_ref[pl.ds(j * LANES, LANES), :] = ((eid + 1) * SLAB).T
        g_ref[pl.ds(j * LANES, LANES), :] = gate_scr[pl.ds(j, PEER_SEL, stride=8), :].T


def _peer_topk(n2b, wqt, sk):
    n = n2b.shape[0]
    tb = min(TOPK_TB, n)
    return pl.pallas_call(
        _peer_topk_kernel,
        grid=(n // tb,),
        in_specs=[pl.BlockSpec((tb, D_MODEL), lambda i: (i, 0)),
                  pl.BlockSpec(wqt.shape, lambda i: (0, 0)),
                  pl.BlockSpec(sk.shape, lambda i: (0, 0, 0, 0))],
        out_specs=[pl.BlockSpec((tb, PEER_SEL), lambda i: (i, 0)),
                   pl.BlockSpec((tb, PEER_SEL), lambda i: (i, 0))],
        out_shape=[jax.ShapeDtypeStruct((n, PEER_SEL), jnp.int32),
                   jax.ShapeDtypeStruct((n, PEER_SEL), jnp.float32)],
        scratch_shapes=[pltpu.VMEM((PEER_HEADS * PEER_D_KEY, tb), jnp.float32),
                        pltpu.VMEM((PEER_N_KEYS, tb), jnp.float32),
                        pltpu.VMEM((PEER_N_KEYS, tb), jnp.float32),
                        pltpu.VMEM((PEER_N_KEYS * 8, LANES), jnp.float32),
                        pltpu.VMEM((PEER_N_KEYS * 8, LANES), jnp.float32),
                        pltpu.VMEM((PEER_SEL * 8, LANES), jnp.int32),
                        pltpu.VMEM((PEER_SEL * 8, LANES), jnp.float32)],
        compiler_params=pltpu.CompilerParams(dimension_semantics=("arbitrary",),
                                             vmem_limit_bytes=48 * 1024 * 1024),
        name="peer_topk",
    )(n2b, wqt, sk)


def _pack_table(t):
    e = t.shape[0]
    tb = t.astype(jnp.bfloat16).reshape(e, 2, SLAB, LANES)
    bits = lax.bitcast_convert_type(tb, jnp.uint16).astype(jnp.uint32)
    word = jnp.pad(bits[:, 0] | (bits[:, 1] << 16), ((1, 1), (0, 0), (0, 0)))
    return lax.bitcast_convert_type(word, jnp.int32).reshape((e + 2) * SLAB, LANES)


IDX_TOKENS = 32
GROUP = 8
PEER_VMEM = 56 * 1024 * 1024


def _index_copy(idx_hbm, ibuf, sem, blk, slot):
    return pltpu.make_async_copy(idx_hbm.at[blk], ibuf.at[slot], sem.at[slot])


def _for_both_index_buffers(idx_hbm, ibuf, sem, process):
    i = pl.program_id(0)

    @pl.when(i == 0)
    def _():
        _index_copy(idx_hbm, ibuf, sem, 0, 0).start()

    _index_copy(idx_hbm, ibuf, sem, 2 * i + 1, 1).start()
    _index_copy(idx_hbm, ibuf, sem, 2 * i, 0).wait()
    process(0)

    @pl.when(i + 1 < pl.num_programs(0))
    def _():
        _index_copy(idx_hbm, ibuf, sem, 2 * i + 2, 0).start()

    _index_copy(idx_hbm, ibuf, sem, 2 * i + 1, 1).wait()
    process(1)


def _gather_matrix(ibuf, slot, base, tab_ref):
    sub = lax.broadcasted_iota(jnp.int32, (2 * SLAB, LANES), 0)
    pairs = []
    for p in range(PEER_SEL // 2):
        ia = pl.multiple_of(ibuf[slot, base + 2 * p], SLAB)
        ib = pl.multiple_of(ibuf[slot, base + 2 * p + 1], SLAB)
        a = tab_ref[pl.ds(ia, 2 * SLAB), :]
        b = tab_ref[pl.ds(ib - SLAB, 2 * SLAB), :]
        pairs.append(jnp.where(sub < SLAB, a, b))
    return pltpu.bitcast(jnp.concatenate(pairs, axis=0), jnp.bfloat16)


def _own_row_mask():
    r = lax.broadcasted_iota(jnp.int32, (ROWS, GATHER_COLS), 0)
    c = lax.broadcasted_iota(jnp.int32, (ROWS, GATHER_COLS), 1)
    return (c % ROWS) == r


def _peer_u_kernel(idx_hbm, tab_ref, xt_ref, g_ref, w_ref, ibuf, sem, p_scr):
    own = _own_row_mask()

    def process(slot):
        for gi in range(IDX_TOKENS // GROUP):
            r0 = slot * IDX_TOKENS + gi * GROUP
            parts = []
            for k in range(GROUP):
                t = _gather_matrix(ibuf, slot, (gi * GROUP + k) * PEER_SEL, tab_ref)
                x8 = xt_ref[pl.ds(ROWS * (r0 + k), ROWS), :].astype(jnp.bfloat16)
                r = lax.dot_general(x8, t, NT_DIMS, preferred_element_type=jnp.float32)
                parts.append(jnp.sum(jnp.where(own, r, 0.0), axis=0, keepdims=True))
            p_scr[pl.ds(r0, GROUP), :] = jnp.concatenate(parts, axis=0)

    _for_both_index_buffers(idx_hbm, ibuf, sem, process)

    col = lax.broadcasted_iota(jnp.int32, (GATHER_COLS, PEER_SEL), 0)
    exp = lax.broadcasted_iota(jnp.int32, (GATHER_COLS, PEER_SEL), 1)
    fold = ((col // ROWS) == exp).astype(jnp.bfloat16)
    hi, lo = _split_hi_lo(p_scr[...])
    act = (jnp.dot(hi, fold, preferred_element_type=jnp.float32)
           + jnp.dot(lo, fold, preferred_element_type=jnp.float32))
    hi, lo = _split_hi_lo(g_ref[...] * jax.nn.gelu(act))
    w_ref[...] = (lax.dot_general(hi, fold, NT_DIMS, preferred_element_type=jnp.float32)
                  + lax.dot_general(lo, fold, NT_DIMS, preferred_element_type=jnp.float32))


RES_STRIDE = 72


def _peer_v_kernel(idx_hbm, tab_ref, w_ref, h_ref, nw_ref, o_ref, ibuf, sem, res):
    own = _own_row_mask()
    tb = h_ref.shape[0]

    def process(slot):
        for gi in range(IDX_TOKENS // GROUP):
            r0 = slot * IDX_TOKENS + gi * GROUP
            wg = w_ref[pl.ds(r0, GROUP), :]
            for k in range(GROUP):
                t = _gather_matrix(ibuf, slot, (gi * GROUP + k) * PEER_SEL, tab_ref)
                w8 = jnp.where(own, wg[k:k + 1], 0.0).astype(jnp.bfloat16)
                res[pl.ds(r0 + k, ROWS, stride=RES_STRIDE), :] = jnp.dot(w8, t, preferred_element_type=jnp.float32)

    _for_both_index_buffers(idx_hbm, ibuf, sem, process)

    planes = [res[pl.ds((2 * (c % SLAB) + c // SLAB) * RES_STRIDE, tb), :] for c in range(ROWS)]
    y = h_ref[...] + jnp.concatenate(planes, axis=-1)
    y = y * lax.rsqrt(jnp.mean(y * y, axis=-1, keepdims=True) + RMS_EPS)
    o_ref[...] = y * nw_ref[...]


def _peer_u(idx_rows, tab, xt, g):
    tb = 2 * IDX_TOKENS
    n = g.shape[0]
    return pl.pallas_call(
        _peer_u_kernel,
        grid=(n // tb,),
        in_specs=[pl.BlockSpec(memory_space=pl.ANY),
                  pl.BlockSpec(memory_space=pltpu.VMEM),
                  pl.BlockSpec((tb * ROWS, LANES), lambda i: (i, 0)),
                  pl.BlockSpec((tb, PEER_SEL), lambda i: (i, 0))],
        out_specs=pl.BlockSpec((tb, GATHER_COLS), lambda i: (i, 0)),
        out_shape=jax.ShapeDtypeStruct((n, GATHER_COLS), jnp.float32),
        scratch_shapes=[pltpu.SMEM((2, IDX_TOKENS * PEER_SEL), jnp.int32),
                        pltpu.SemaphoreType.DMA((2,)),
                        pltpu.VMEM((tb, GATHER_COLS), jnp.float32)],
        compiler_params=pltpu.CompilerParams(dimension_semantics=("arbitrary",), vmem_limit_bytes=PEER_VMEM),
        name="peer_u",
    )(idx_rows, tab, xt, g)


def _peer_v(idx_rows, tab, w, h, nw):
    tb = 2 * IDX_TOKENS
    n = h.shape[0]
    return pl.pallas_call(
        _peer_v_kernel,
        grid=(n // tb,),
        in_specs=[pl.BlockSpec(memory_space=pl.ANY),
                  pl.BlockSpec(memory_space=pltpu.VMEM),
                  pl.BlockSpec((tb, GATHER_COLS), lambda i: (i, 0)),
                  pl.BlockSpec((tb, D_MODEL), lambda i: (i, 0)),
                  pl.BlockSpec((1, D_MODEL), lambda i: (0, 0))],
        out_specs=pl.BlockSpec((tb, D_MODEL), lambda i: (i, 0)),
        out_shape=jax.ShapeDtypeStruct((n, D_MODEL), jnp.float32),
        scratch_shapes=[pltpu.SMEM((2, IDX_TOKENS * PEER_SEL), jnp.int32),
                        pltpu.SemaphoreType.DMA((2,)),
                        pltpu.VMEM((ROWS * RES_STRIDE, LANES), jnp.float32)],
        compiler_params=pltpu.CompilerParams(dimension_semantics=("arbitrary",), vmem_limit_bytes=PEER_VMEM),
        name="peer_v",
    )(idx_rows, tab, w, h, nw)


def _peer_block(xt, n2b, h, w_query, sub_keys, u, v, norm_final_w):
    n = h.shape[0]
    idx, g = _peer_topk(n2b, w_query.T.astype(jnp.bfloat16), sub_keys.astype(jnp.bfloat16))
    idx_rows = idx.reshape(n // IDX_TOKENS, IDX_TOKENS * PEER_SEL)
    w = _peer_u(idx_rows, _pack_table(u), xt, g)
    return _peer_v(idx_rows, _pack_table(v), w, h, norm_final_w.reshape(1, D_MODEL))


def kernel(x, norm_mix_w, w_in, hgrn_lb_param, hgrn_norm_w, diff_lambda, diff_norm_w,
           w_out, norm_ffn_w, peer_w_query, peer_sub_keys, peer_u, peer_v, norm_final_w):
    bsz, seq, d = x.shape
    n = bsz * seq
    x2 = x.reshape(n, d)
    proj_hg, proj_da = _in_proj(x2, norm_mix_w[0], w_in[0].astype(jnp.bfloat16))
    hg_out = _hgrn2_pallas(proj_hg.reshape(bsz, seq, -1), hgrn_lb_param, hgrn_norm_w[0])
    proj_da = proj_da.reshape(bsz, seq, -1)
    da_out = _diff_attn_pallas(proj_da, proj_da, proj_da, diff_lambda[0], diff_norm_w[0],
                               cols=(0, DA_HEADS, 2 * DA_HEADS))
    h, xt, n2b = _out_proj(hg_out.reshape(n, -1), da_out.reshape(n, -1), w_out[0].astype(jnp.bfloat16),
                           x2, norm_ffn_w[0])
    out = _peer_block(xt, n2b, h, peer_w_query[0], peer_sub_keys[0], peer_u[0], peer_v[0], norm_final_w)
    return out.reshape(bsz, seq, d)
```

```python
import math
import jax, jax.numpy as jnp
from jax import lax
from jax.experimental import pallas as pl
from jax.experimental.pallas import tpu as pltpu

D_MODEL = 1024
CHUNK = 64
HG_HEADS = 4
HG_DK = 128
HG_DV = 128
DA_HEADS = 4
DA_DV = 128
DA_DH = 64
PEER_HEADS = 8
PEER_N_KEYS = 128
PEER_D_KEY = 256
PEER_HALF = 128
PEER_TOPK = 16
RMS_EPS = 1e-6
LANES = 128
HG_COLS = HG_HEADS * (2 * HG_DK + 2 * HG_DV)
NT_DIMS = (((1,), (1,)), ((), ()))


def _split_hi_lo(a):
    hi = a.astype(jnp.bfloat16)
    return hi, (a - hi.astype(jnp.float32)).astype(jnp.bfloat16)


PROJ_TM = 512
PROJ_VMEM = 48 * 1024 * 1024


def _in_proj_kernel(x_ref, nw_ref, w_ref, hg_ref, da_ref):
    x = x_ref[...]
    n = (x * lax.rsqrt(jnp.mean(x * x, axis=-1, keepdims=True) + RMS_EPS) * nw_ref[...]).astype(jnp.bfloat16)
    hg_ref[...] = jnp.dot(n, w_ref[:, :HG_COLS], preferred_element_type=jnp.float32)
    da_ref[...] = jnp.dot(n, w_ref[:, HG_COLS:], preferred_element_type=jnp.float32).astype(da_ref.dtype)


def _in_proj(x, nw, w):
    n, d = x.shape
    e = w.shape[1]
    tm = min(PROJ_TM, n)
    return pl.pallas_call(
        _in_proj_kernel,
        grid=(n // tm,),
        in_specs=[pl.BlockSpec((tm, d), lambda i: (i, 0)),
                  pl.BlockSpec((1, d), lambda i: (0, 0)),
                  pl.BlockSpec((d, e), lambda i: (0, 0))],
        out_specs=[pl.BlockSpec((tm, HG_COLS), lambda i: (i, 0)),
                   pl.BlockSpec((tm, e - HG_COLS), lambda i: (i, 0))],
        out_shape=[jax.ShapeDtypeStruct((n, HG_COLS), jnp.float32),
                   jax.ShapeDtypeStruct((n, e - HG_COLS), jnp.bfloat16)],
        compiler_params=pltpu.CompilerParams(dimension_semantics=("arbitrary",), vmem_limit_bytes=PROJ_VMEM),
        name="in_proj",
    )(x, nw.reshape(1, d), w)


SLAB = D_MODEL // 2 // LANES
HALF_D = D_MODEL // 2
SUBLANES = 8
PAD_EXPERTS = 256
ROWS = 2 * SLAB


def _out_proj_kernel(hg_ref, da_ref, w_ref, x_ref, nw_ref, h_ref, xt_ref, nb_ref):
    tm = x_ref.shape[0]
    half = hg_ref.shape[1]
    h = (x_ref[...]
         + jnp.dot(hg_ref[...], w_ref[pl.ds(0, half), :], preferred_element_type=jnp.float32)
         + jnp.dot(da_ref[...], w_ref[pl.ds(half, half), :], preferred_element_type=jnp.float32))
    h_ref[...] = h
    n = h * lax.rsqrt(jnp.mean(h * h, axis=-1, keepdims=True) + RMS_EPS) * nw_ref[...]
    nb_ref[...] = n.astype(jnp.bfloat16)
    for c in range(ROWS):
        r = 2 * (c % SLAB) + c // SLAB
        xt_ref[pl.ds(r, tm, stride=ROWS), :] = n[:, c * LANES:(c + 1) * LANES]


def _out_proj(hg, da, w, x, nw):
    n, d = x.shape
    half = hg.shape[1]
    tm = min(PROJ_TM, n)
    row = lambda cols: pl.BlockSpec((tm, cols), lambda i: (i, 0))
    return pl.pallas_call(
        _out_proj_kernel,
        grid=(n // tm,),
        in_specs=[row(half), row(half), pl.BlockSpec((2 * half, d), lambda i: (0, 0)), row(d),
                  pl.BlockSpec((1, d), lambda i: (0, 0))],
        out_specs=[row(d), pl.BlockSpec((tm * ROWS, LANES), lambda i: (i, 0)), row(d)],
        out_shape=[jax.ShapeDtypeStruct((n, d), jnp.float32),
                   jax.ShapeDtypeStruct((n * ROWS, LANES), jnp.float32),
                   jax.ShapeDtypeStruct((n, d), jnp.bfloat16)],
        compiler_params=pltpu.CompilerParams(dimension_semantics=("arbitrary",), vmem_limit_bytes=PROJ_VMEM),
        name="out_proj",
    )(hg, da, w, x, nw.reshape(1, d))


SUB = 16
HG_SEQ_BLOCK = 512


def _hgrn2_kernel(q_ref, f_ref, v_ref, g_ref, lb_ref, nw_ref, o_ref, state_t):
    @pl.when(pl.program_id(1) == 0)
    def _():
        state_t[...] = jnp.zeros_like(state_t)

    lp = lb_ref[...]
    le = jnp.exp(lp - jnp.max(lp, axis=0, keepdims=True))
    lb_all = le[0:1] / jnp.sum(le, axis=0, keepdims=True)
    nw = nw_ref[...]
    r64 = lax.broadcasted_iota(jnp.int32, (CHUNK, CHUNK), 0)
    c64 = lax.broadcasted_iota(jnp.int32, (CHUNK, CHUNK), 1)
    tri = (c64 <= r64).astype(jnp.bfloat16)
    row_k = lax.broadcasted_iota(jnp.int32, (CHUNK, HG_DK), 0)
    row_s = lax.broadcasted_iota(jnp.int32, (SUB, HG_DK), 0)
    lane_s = lax.broadcasted_iota(jnp.int32, (SUB, CHUNK), 1)

    def head_chunk(hd, r0):
        cols = pl.ds(hd * HG_DK, HG_DK)
        lb = lb_all[:, hd * HG_DK:(hd + 1) * HG_DK]
        f = lb + (1.0 - lb) * jax.nn.sigmoid(f_ref[pl.ds(r0, CHUNK), cols])
        k = 1.0 - f
        hi, lo = _split_hi_lo(jnp.log(f))
        b = (jnp.dot(tri, hi, preferred_element_type=jnp.float32)
             + jnp.dot(tri, lo, preferred_element_type=jnp.float32))
        q = q_ref[pl.ds(r0, CHUNK), cols] * (HG_DK ** -0.5)
        v = v_ref[pl.ds(r0, CHUNK), cols]
        strips = []
        for i in range(CHUNK // SUB):
            lo_r = i * SUB
            bi, qi, ki = b[lo_r:lo_r + SUB], q[lo_r:lo_r + SUB], k[lo_r:lo_r + SUB]
            diag = jnp.zeros((SUB, CHUNK), jnp.float32)
            for s in range(SUB):
                dec = jnp.exp(jnp.where(row_s >= s, bi - bi[s:s + 1], -jnp.inf))
                col = jnp.sum(qi * ki[s:s + 1] * dec, axis=-1, keepdims=True)
                diag = jnp.where(lane_s == lo_r + s, col, diag)
            if i == 0:
                strips.append(diag)
            else:
                ref = b[lo_r - 1:lo_r]
                qt = (qi * jnp.exp(bi - ref)).astype(jnp.bfloat16)
                kt = (k * jnp.exp(jnp.where(row_k < lo_r, ref - b, -jnp.inf))).astype(jnp.bfloat16)
                strips.append(diag + lax.dot_general(qt, kt, NT_DIMS, preferred_element_type=jnp.float32))
        a = jnp.concatenate(strips, axis=0).astype(jnp.bfloat16)
        st = state_t[hd]
        o = jnp.dot(a, v.astype(jnp.bfloat16), preferred_element_type=jnp.float32)
        o = o + lax.dot_general((q * jnp.exp(b)).astype(jnp.bfloat16), st.astype(jnp.bfloat16), NT_DIMS,
                                preferred_element_type=jnp.float32)
        b_last = b[CHUNK - 1:CHUNK]
        kd = (k * jnp.exp(b_last - b)).astype(jnp.bfloat16)
        state_t[hd] = st * jnp.exp(b_last) + jnp.dot(v.T.astype(jnp.bfloat16), kd,
                                                     preferred_element_type=jnp.float32)
        y = o * lax.rsqrt(jnp.mean(o * o, axis=-1, keepdims=True) + RMS_EPS) * nw
        o_ref[pl.ds(r0, CHUNK), cols] = (y * jax.nn.silu(g_ref[pl.ds(r0, CHUNK), cols])).astype(o_ref.dtype)

    def chunk(ci, carry):
        r0 = pl.multiple_of(ci * CHUNK, CHUNK)
        for hd in range(HG_HEADS):
            head_chunk(hd, r0)
        return carry

    lax.fori_loop(0, q_ref.shape[0] // CHUNK, chunk, 0)


def _hgrn2_pallas(proj, lb_param, norm_w):
    bsz, seq, _ = proj.shape
    sb = min(HG_SEQ_BLOCK, seq)
    nl = lb_param.shape[0]
    width = HG_HEADS * HG_DK
    col = lambda group: pl.BlockSpec((None, sb, width), lambda b, s: (b, s, group))
    return pl.pallas_call(
        _hgrn2_kernel,
        grid=(bsz, seq // sb),
        in_specs=[col(0), col(1), col(2), col(3),
                  pl.BlockSpec((nl, width), lambda b, s: (0, 0)),
                  pl.BlockSpec((1, HG_DV), lambda b, s: (0, 0))],
        out_specs=pl.BlockSpec((None, sb, width), lambda b, s: (b, s, 0)),
        out_shape=jax.ShapeDtypeStruct((bsz, seq, width), jnp.bfloat16),
        scratch_shapes=[pltpu.VMEM((HG_HEADS, HG_DV, HG_DK), jnp.float32)],
        compiler_params=pltpu.CompilerParams(dimension_semantics=("arbitrary", "arbitrary")),
        name="hgrn2",
    )(proj, proj, proj, proj, lb_param, norm_w.reshape(1, HG_DV))


DA_TILE = 512
LAMBDA_INIT = 0.8 - 0.6 * math.exp(-0.3 * 0)


def _diff_attn_kernel(q_ref, k_ref, v_ref, lp_ref, nw_ref, o_ref, m_scr, l_scr, acc_scr):
    t = q_ref.shape[0]
    qi = pl.program_id(2)
    lane = lax.broadcasted_iota(jnp.int32, (t, 2 * DA_DH), 1)
    q = q_ref[...] * jnp.asarray(DA_DH ** -0.5, jnp.bfloat16)
    qs = [jnp.where(lane < DA_DH, q, jnp.zeros_like(q)), jnp.where(lane >= DA_DH, q, jnp.zeros_like(q))]
    m_scr[...] = jnp.full(m_scr.shape, -jnp.inf, jnp.float32)
    l_scr[...] = jnp.zeros_like(l_scr)
    acc_scr[...] = jnp.zeros_like(acc_scr)

    def sweep(kj, mask):
        k0 = pl.multiple_of(kj * t, t)
        kt = k_ref[pl.ds(k0, t), :]
        vt = v_ref[pl.ds(k0, t), :]
        for c in range(2):
            s = lax.dot_general(qs[c], kt, NT_DIMS, preferred_element_type=jnp.float32)
            if mask is not None:
                s = jnp.where(mask, s, -jnp.inf)
            m_old = m_scr[c]
            m_new = jnp.maximum(m_old, jnp.max(s, axis=-1, keepdims=True))
            alpha = jnp.exp(m_old - m_new)
            p = jnp.exp(s - jnp.tile(m_new, (1, t // LANES)))
            l_scr[c] = alpha * l_scr[c] + jnp.sum(p, axis=-1, keepdims=True)
            acc_scr[c] = alpha * acc_scr[c] + jnp.dot(p.astype(jnp.bfloat16), vt,
                                                      preferred_element_type=jnp.float32)
            m_scr[c] = m_new

    def full_tile(kj, carry):
        sweep(kj, None)
        return carry

    lax.fori_loop(0, qi, full_tile, 0)
    row = lax.broadcasted_iota(jnp.int32, (t, t), 0)
    col = lax.broadcasted_iota(jnp.int32, (t, t), 1)
    sweep(qi, (col // CHUNK) <= (row // CHUNK))

    lp = lp_ref[...]
    lam = (jnp.exp(jnp.sum(lp[0:1] * lp[1:2], axis=-1, keepdims=True))
           - jnp.exp(jnp.sum(lp[2:3] * lp[3:4], axis=-1, keepdims=True)) + LAMBDA_INIT)
    o = acc_scr[0] / l_scr[0] - lam * (acc_scr[1] / l_scr[1])
    y = o * lax.rsqrt(jnp.mean(o * o, axis=-1, keepdims=True) + RMS_EPS) * nw_ref[...]
    o_ref[...] = (y * (1.0 - LAMBDA_INIT)).astype(o_ref.dtype)


def _diff_attn_pallas(q, k, v, lam_params, norm_w, cols=(0, 0, 0)):
    bsz, seq, _ = q.shape
    t = min(DA_TILE, seq)
    cq, ck, cv = cols
    return pl.pallas_call(
        _diff_attn_kernel,
        grid=(bsz, DA_HEADS, seq // t),
        in_specs=[pl.BlockSpec((None, t, 2 * DA_DH), lambda b, h, i: (b, i, cq + h)),
                  pl.BlockSpec((None, seq, 2 * DA_DH), lambda b, h, i: (b, 0, ck + h)),
                  pl.BlockSpec((None, seq, DA_DV), lambda b, h, i: (b, 0, cv + h)),
                  pl.BlockSpec((4, DA_DH), lambda b, h, i: (0, 0)),
                  pl.BlockSpec((1, DA_DV), lambda b, h, i: (0, 0))],
        out_specs=pl.BlockSpec((None, t, DA_DV), lambda b, h, i: (b, i, h)),
        out_shape=jax.ShapeDtypeStruct((bsz, seq, DA_HEADS * DA_DV), jnp.bfloat16),
        scratch_shapes=[pltpu.VMEM((2, t, LANES), jnp.float32), pltpu.VMEM((2, t, LANES), jnp.float32),
                        pltpu.VMEM((2, t, DA_DV), jnp.float32)],
        compiler_params=pltpu.CompilerParams(dimension_semantics=("arbitrary", "arbitrary", "arbitrary")),
        name="diff_attn",
    )(q, k, v, lam_params, norm_w.reshape(1, DA_DV))


PEER_SEL = PEER_HEADS * PEER_TOPK
GATHER_COLS = PEER_SEL * ROWS
_CAND_NB = [PEER_TOPK // (a + 1) for a in range(PEER_TOPK)]
_CAND_ROWS = 56
TOPK_TB = SUBLANES * LANES


def _topk_rows(s, k):
    n = s.shape[0]
    iota = lax.broadcasted_iota(jnp.int32, s.shape, 0)
    vals, idxs = [], []
    for _ in range(k):
        m = jnp.max(s, axis=0, keepdims=True)
        idx = jnp.min(jnp.where(s == m, iota, n), axis=0, keepdims=True)
        s = jnp.where(iota == idx, -jnp.inf, s)
        vals.append(m)
        idxs.append(idx)
    return jnp.concatenate(vals, axis=0), jnp.concatenate(idxs, axis=0)


def _select_exact(s1, s2):
    v1, i1 = _topk_rows(s1, PEER_TOPK)
    v2, i2 = _topk_rows(s2, PEER_TOPK)
    tb = v1.shape[1]
    cand = jnp.concatenate(
        [v1[a:a + 1] + v2[:_CAND_NB[a]] for a in range(PEER_TOPK)]
        + [jnp.full((_CAND_ROWS - sum(_CAND_NB), tb), -jnp.inf, jnp.float32)], axis=0)
    cidx = jnp.concatenate(
        [i1[a:a + 1] * PEER_N_KEYS + i2[:_CAND_NB[a]] for a in range(PEER_TOPK)]
        + [jnp.zeros((_CAND_ROWS - sum(_CAND_NB), tb), jnp.int32)], axis=0)
    ts, pos = _topk_rows(cand, PEER_TOPK)
    iota = lax.broadcasted_iota(jnp.int32, cand.shape, 0)
    eid = jnp.concatenate(
        [jnp.sum(jnp.where(iota == pos[r:r + 1], cidx, 0), axis=0, keepdims=True)
         for r in range(PEER_TOPK)], axis=0)
    e = jnp.exp(ts - ts[0:1])
    return eid, e / jnp.sum(e, axis=0, keepdims=True)


def _sort16_pairs():
    n, pairs, p = 16, [], 1
    while p < n:
        k = p
        while k >= 1:
            for j in range(k % p, n - k, 2 * k):
                for i in range(min(k, n - j - k)):
                    if (i + j) // (2 * p) == (i + j + k) // (2 * p):
                        pairs.append((i + j, i + j + k))
            k //= 2
        p *= 2
    return pairs


_SORT16 = _sort16_pairs()


def _exchange(a, b):
    (va, ia), (vb, ib) = a, b
    swap = vb > va
    return (jnp.maximum(va, vb), jnp.where(swap, ib, ia)), (jnp.minimum(va, vb), jnp.where(swap, ia, ib))


def _top16_sorted(items):
    groups = []
    for g0 in range(0, len(items), 16):
        grp = list(items[g0:g0 + 16])
        for i, j in _SORT16:
            grp[i], grp[j] = _exchange(grp[i], grp[j])
        groups.append(grp)
    dropped = None
    while len(groups) > 1:
        merged = []
        for a, b in zip(groups[0::2], groups[1::2]):
            top = []
            for i in range(16):
                hi, lo = _exchange(a[i], b[15 - i])
                top.append(hi)
                dropped = lo[0] if dropped is None else jnp.maximum(dropped, lo[0])
            d = 8
            while d >= 1:
                for i in range(16):
                    if (i & d) == 0:
                        top[i], top[i + d] = _exchange(top[i], top[i + d])
                d //= 2
            merged.append(top)
        groups = merged
    return groups[0], dropped


def _has_tie(top, dropped):
    tie = top[PEER_TOPK - 1][0] == dropped
    for i in range(PEER_TOPK - 1):
        tie = tie | (top[i][0] == top[i + 1][0])
    return tie


def _peer_topk_kernel(n2_ref, wqt_ref, sk_ref, idx_ref, g_ref, qh_scr, s1_scr, s2_scr, k1_scr, k2_scr,
                      eid_scr, gate_scr):
    qh_scr[...] = lax.dot_general(wqt_ref[...], n2_ref[...], NT_DIMS, preferred_element_type=jnp.float32)
    groups = n2_ref.shape[0] // LANES

    def key_tiles(k_scr, n):
        return [(k_scr[pl.ds(SUBLANES * k, SUBLANES), :], jnp.full((SUBLANES, LANES), k, jnp.int32)) for k in range(n)]

    def head(h, carry):
        base = pl.multiple_of(h * PEER_D_KEY, PEER_D_KEY)
        q1 = qh_scr[pl.ds(base, PEER_HALF), :].astype(jnp.bfloat16)
        q2 = qh_scr[pl.ds(base + PEER_HALF, PEER_HALF), :].astype(jnp.bfloat16)
        s1_scr[...] = jnp.dot(sk_ref[h, 0], q1, preferred_element_type=jnp.float32)
        s2_scr[...] = jnp.dot(sk_ref[h, 1], q2, preferred_element_type=jnp.float32)
        for j in range(groups):
            k1_scr[pl.ds(j, PEER_N_KEYS, stride=SUBLANES), :] = s1_scr[:, j * LANES:(j + 1) * LANES]
            k2_scr[pl.ds(j, PEER_N_KEYS, stride=SUBLANES), :] = s2_scr[:, j * LANES:(j + 1) * LANES]

        top1, drop1 = _top16_sorted(key_tiles(k1_scr, PEER_N_KEYS))
        top2, drop2 = _top16_sorted(key_tiles(k2_scr, PEER_N_KEYS))
        cands = [(top1[a][0] + top2[b][0], top1[a][1] * PEER_N_KEYS + top2[b][1])
                 for a in range(PEER_TOPK) for b in range(_CAND_NB[a])]
        pad = (jnp.full((SUBLANES, LANES), -jnp.inf, jnp.float32), jnp.zeros((SUBLANES, LANES), jnp.int32))
        cands += [pad] * (64 - len(cands))
        top, dropc = _top16_sorted(cands)
        es = [jnp.exp(t[0] - top[0][0]) for t in top]
        den = es[0]
        for e in es[1:]:
            den = den + e
        slot0 = h * PEER_TOPK
        for r in range(PEER_TOPK):
            row = pl.multiple_of((slot0 + r) * SUBLANES, SUBLANES)
            eid_scr[pl.ds(row, SUBLANES), :] = top[r][1]
            gate_scr[pl.ds(row, SUBLANES), :] = es[r] / den

        tie = _has_tie(top1, drop1) | _has_tie(top2, drop2) | _has_tie(top, dropc)

        @pl.when(jnp.any(tie))
        def _():
            eid, gate = _select_exact(s1_scr[...], s2_scr[...])
            for r in range(PEER_TOPK):
                row = pl.multiple_of((slot0 + r) * SUBLANES, SUBLANES)
                eid_scr[pl.ds(row, SUBLANES), :] = jnp.concatenate(
                    [eid[r:r + 1, j * LANES:(j + 1) * LANES] for j in range(groups)], axis=0)
                gate_scr[pl.ds(row, SUBLANES), :] = jnp.concatenate(
                    [gate[r:r + 1, j * LANES:(j + 1) * LANES] for j in range(groups)], axis=0)

        return carry

    lax.fori_loop(0, PEER_HEADS, head, 0)
    for j in range(groups):
        eid = eid_scr[pl.ds(j, PEER_SEL, stride=SUBLANES), :]
        idx_ref[pl.ds(j * LANES, LANES), :] = ((eid + PAD_EXPERTS) * SLAB).T
        g_ref[pl.ds(j * LANES, LANES), :] = gate_scr[pl.ds(j, PEER_SEL, stride=SUBLANES), :].T


def _peer_topk(n2b, wqt, sk):
    n = n2b.shape[0]
    tb = min(TOPK_TB, n)
    return pl.pallas_call(
        _peer_topk_kernel,
        grid=(n // tb,),
        in_specs=[pl.BlockSpec((tb, D_MODEL), lambda i: (i, 0)),
                  pl.BlockSpec(wqt.shape, lambda i: (0, 0)),
                  pl.BlockSpec(sk.shape, lambda i: (0, 0, 0, 0))],
        out_specs=[pl.BlockSpec((tb, PEER_SEL), lambda i: (i, 0)),
                   pl.BlockSpec((tb, PEER_SEL), lambda i: (i, 0))],
        out_shape=[jax.ShapeDtypeStruct((n, PEER_SEL), jnp.int32),
                   jax.ShapeDtypeStruct((n, PEER_SEL), jnp.float32)],
        scratch_shapes=[pltpu.VMEM((PEER_HEADS * PEER_D_KEY, tb), jnp.float32),
                        pltpu.VMEM((PEER_N_KEYS, tb), jnp.float32),
                        pltpu.VMEM((PEER_N_KEYS, tb), jnp.float32),
                        pltpu.VMEM((PEER_N_KEYS * SUBLANES, LANES), jnp.float32),
                        pltpu.VMEM((PEER_N_KEYS * SUBLANES, LANES), jnp.float32),
                        pltpu.VMEM((PEER_SEL * SUBLANES, LANES), jnp.int32),
                        pltpu.VMEM((PEER_SEL * SUBLANES, LANES), jnp.float32)],
        compiler_params=pltpu.CompilerParams(dimension_semantics=("arbitrary",),
                                             vmem_limit_bytes=48 * 1024 * 1024),
        name="peer_topk",
    )(n2b, wqt, sk)


def _pack_kernel(t_ref, o_ref):
    i = pl.program_id(0)
    edge = (i == 0) | (i == pl.num_programs(0) - 1)

    @pl.when(edge)
    def _():
        o_ref[...] = jnp.zeros_like(o_ref)

    @pl.when(jnp.logical_not(edge))
    def _():
        rounded = t_ref[...].astype(jnp.bfloat16).astype(jnp.float32)
        bits = pltpu.bitcast(rounded, jnp.uint32)
        for j in range(SLAB):
            lo = bits[:, j * LANES:(j + 1) * LANES] >> 16
            hi = bits[:, HALF_D + j * LANES:HALF_D + (j + 1) * LANES]
            o_ref[pl.ds(j, PAD_EXPERTS, stride=SLAB), :] = pltpu.bitcast(lo | hi, jnp.int32)


def _pack_table(t):
    e, d = t.shape
    blocks = e // PAD_EXPERTS
    return pl.pallas_call(
        _pack_kernel,
        grid=(blocks + 2,),
        in_specs=[pl.BlockSpec((PAD_EXPERTS, d), lambda i: (jnp.clip(i - 1, 0, blocks - 1), 0))],
        out_specs=pl.BlockSpec((PAD_EXPERTS * SLAB, LANES), lambda i: (i, 0)),
        out_shape=jax.ShapeDtypeStruct(((e + 2 * PAD_EXPERTS) * SLAB, LANES), jnp.int32),
        compiler_params=pltpu.CompilerParams(dimension_semantics=("arbitrary",)),
        name="pack_table",
    )(t)


IDX_TOKENS = 32
GROUP = 8
PEER_VMEM = 56 * 1024 * 1024


def _index_copy(idx_hbm, ibuf, sem, blk, slot):
    return pltpu.make_async_copy(idx_hbm.at[blk], ibuf.at[slot], sem.at[slot])


def _for_both_index_buffers(idx_hbm, ibuf, sem, process):
    i = pl.program_id(0)

    @pl.when(i == 0)
    def _():
        _index_copy(idx_hbm, ibuf, sem, 0, 0).start()

    _index_copy(idx_hbm, ibuf, sem, 2 * i + 1, 1).start()
    _index_copy(idx_hbm, ibuf, sem, 2 * i, 0).wait()
    process(0)

    @pl.when(i + 1 < pl.num_programs(0))
    def _():
        _index_copy(idx_hbm, ibuf, sem, 2 * i + 2, 0).start()

    _index_copy(idx_hbm, ibuf, sem, 2 * i + 1, 1).wait()
    process(1)


def _gather_matrix(ibuf, slot, base, tab_ref):
    sub = lax.broadcasted_iota(jnp.int32, (2 * SLAB, LANES), 0)
    pairs = []
    for p in range(PEER_SEL // 2):
        ia = pl.multiple_of(ibuf[slot, base + 2 * p], SLAB)
        ib = pl.multiple_of(ibuf[slot, base + 2 * p + 1], SLAB)
        a = tab_ref[pl.ds(ia, 2 * SLAB), :]
        b = tab_ref[pl.ds(ib - SLAB, 2 * SLAB), :]
        pairs.append(jnp.where(sub < SLAB, a, b))
    return pltpu.bitcast(jnp.concatenate(pairs, axis=0), jnp.bfloat16)


def _own_row_mask():
    r = lax.broadcasted_iota(jnp.int32, (ROWS, GATHER_COLS), 0)
    c = lax.broadcasted_iota(jnp.int32, (ROWS, GATHER_COLS), 1)
    return (c % ROWS) == r


def _peer_u_kernel(idx_hbm, tab_ref, xt_ref, g_ref, w_ref, ibuf, sem, p_scr):
    own = _own_row_mask()

    def process(slot):
        for gi in range(IDX_TOKENS // GROUP):
            r0 = slot * IDX_TOKENS + gi * GROUP
            parts = []
            for k in range(GROUP):
                t = _gather_matrix(ibuf, slot, (gi * GROUP + k) * PEER_SEL, tab_ref)
                x8 = xt_ref[pl.ds(ROWS * (r0 + k), ROWS), :].astype(jnp.bfloat16)
                r = lax.dot_general(x8, t, NT_DIMS, preferred_element_type=jnp.float32)
                parts.append(jnp.sum(jnp.where(own, r, 0.0), axis=0, keepdims=True))
            p_scr[pl.ds(r0, GROUP), :] = jnp.concatenate(parts, axis=0)

    _for_both_index_buffers(idx_hbm, ibuf, sem, process)

    col = lax.broadcasted_iota(jnp.int32, (GATHER_COLS, PEER_SEL), 0)
    exp = lax.broadcasted_iota(jnp.int32, (GATHER_COLS, PEER_SEL), 1)
    fold = ((col // ROWS) == exp).astype(jnp.bfloat16)
    hi, lo = _split_hi_lo(p_scr[...])
    act = (jnp.dot(hi, fold, preferred_element_type=jnp.float32)
           + jnp.dot(lo, fold, preferred_element_type=jnp.float32))
    hi, lo = _split_hi_lo(g_ref[...] * jax.nn.gelu(act))
    w_ref[...] = (lax.dot_general(hi, fold, NT_DIMS, preferred_element_type=jnp.float32)
                  + lax.dot_general(lo, fold, NT_DIMS, preferred_element_type=jnp.float32))


RES_STRIDE = 72


def _peer_v_kernel(idx_hbm, tab_ref, w_ref, h_ref, nw_ref, o_ref, ibuf, sem, res):
    own = _own_row_mask()
    tb = h_ref.shape[0]

    def process(slot):
        for gi in range(IDX_TOKENS // GROUP):
            r0 = slot * IDX_TOKENS + gi * GROUP
            wg = w_ref[pl.ds(r0, GROUP), :]
            for k in range(GROUP):
                t = _gather_matrix(ibuf, slot, (gi * GROUP + k) * PEER_SEL, tab_ref)
                w8 = jnp.where(own, wg[k:k + 1], 0.0).astype(jnp.bfloat16)
                res[pl.ds(r0 + k, ROWS, stride=RES_STRIDE), :] = jnp.dot(w8, t, preferred_element_type=jnp.float32)

    _for_both_index_buffers(idx_hbm, ibuf, sem, process)

    planes = [res[pl.ds((2 * (c % SLAB) + c // SLAB) * RES_STRIDE, tb), :] for c in range(ROWS)]
    y = h_ref[...] + jnp.concatenate(planes, axis=-1)
    y = y * lax.rsqrt(jnp.mean(y * y, axis=-1, keepdims=True) + RMS_EPS)
    o_ref[...] = y * nw_ref[...]


def _peer_u(idx_rows, tab, xt, g):
    tb = 2 * IDX_TOKENS
    n = g.shape[0]
    return pl.pallas_call(
        _peer_u_kernel,
        grid=(n // tb,),
        in_specs=[pl.BlockSpec(memory_space=pl.ANY),
                  pl.BlockSpec(memory_space=pltpu.VMEM),
                  pl.BlockSpec((tb * ROWS, LANES), lambda i: (i, 0)),
                  pl.BlockSpec((tb, PEER_SEL), lambda i: (i, 0))],
        out_specs=pl.BlockSpec((tb, GATHER_COLS), lambda i: (i, 0)),
        out_shape=jax.ShapeDtypeStruct((n, GATHER_COLS), jnp.float32),
        scratch_shapes=[pltpu.SMEM((2, IDX_TOKENS * PEER_SEL), jnp.int32),
                        pltpu.SemaphoreType.DMA((2,)),
                        pltpu.VMEM((tb, GATHER_COLS), jnp.float32)],
        compiler_params=pltpu.CompilerParams(dimension_semantics=("arbitrary",), vmem_limit_bytes=PEER_VMEM),
        name="peer_u",
    )(idx_rows, tab, xt, g)


def _peer_v(idx_rows, tab, w, h, nw):
    tb = 2 * IDX_TOKENS
    n = h.shape[0]
    return pl.pallas_call(
        _peer_v_kernel,
        grid=(n // tb,),
        in_specs=[pl.BlockSpec(memory_space=pl.ANY),
                  pl.BlockSpec(memory_space=pltpu.VMEM),
                  pl.BlockSpec((tb, GATHER_COLS), lambda i: (i, 0)),
                  pl.BlockSpec((tb, D_MODEL), lambda i: (i, 0)),
                  pl.BlockSpec((1, D_MODEL), lambda i: (0, 0))],
        out_specs=pl.BlockSpec((tb, D_MODEL), lambda i: (i, 0)),
        out_shape=jax.ShapeDtypeStruct((n, D_MODEL), jnp.float32),
        scratch_shapes=[pltpu.SMEM((2, IDX_TOKENS * PEER_SEL), jnp.int32),
                        pltpu.SemaphoreType.DMA((2,)),
                        pltpu.VMEM((ROWS * RES_STRIDE, LANES), jnp.float32)],
        compiler_params=pltpu.CompilerParams(dimension_semantics=("arbitrary",), vmem_limit_bytes=PEER_VMEM),
        name="peer_v",
    )(idx_rows, tab, w, h, nw)


def _peer_block(xt, n2b, h, w_query, sub_keys, u, v, norm_final_w):
    n = h.shape[0]
    idx, g = _peer_topk(n2b, w_query.T.astype(jnp.bfloat16), sub_keys.astype(jnp.bfloat16))
    idx_rows = idx.reshape(n // IDX_TOKENS, IDX_TOKENS * PEER_SEL)
    w = _peer_u(idx_rows, _pack_table(u), xt, g)
    return _peer_v(idx_rows, _pack_table(v), w, h, norm_final_w.reshape(1, D_MODEL))


def kernel(x, norm_mix_w, w_in, hgrn_lb_param, hgrn_norm_w, diff_lambda, diff_norm_w,
           w_out, norm_ffn_w, peer_w_query, peer_sub_keys, peer_u, peer_v, norm_final_w):
    bsz, seq, d = x.shape
    n = bsz * seq
    x2 = x.reshape(n, d)
    proj_hg, proj_da = _in_proj(x2, norm_mix_w[0], w_in[0].astype(jnp.bfloat16))
    hg_out = _hgrn2_pallas(proj_hg.reshape(bsz, seq, -1), hgrn_lb_param, hgrn_norm_w[0])
    proj_da = proj_da.reshape(bsz, seq, -1)
    da_out = _diff_attn_pallas(proj_da, proj_da, proj_da, diff_lambda[0], diff_norm_w[0],
                               cols=(0, DA_HEADS, 2 * DA_HEADS))
    h, xt, n2b = _out_proj(hg_out.reshape(n, -1), da_out.reshape(n, -1), w_out[0].astype(jnp.bfloat16),
                           x2, norm_ffn_w[0])
    out = _peer_block(xt, n2b, h, peer_w_query[0], peer_sub_keys[0], peer_u[0], peer_v[0], norm_final_w)
    return out.reshape(bsz, seq, d)
```

```python
import math
import jax, jax.numpy as jnp
from jax import lax
from jax.experimental import pallas as pl
from jax.experimental.pallas import tpu as pltpu

D_MODEL = 1024
CHUNK = 64
HG_HEADS = 4
HG_DK = 128
HG_DV = 128
DA_HEADS = 4
DA_DV = 128
DA_DH = 64
PEER_HEADS = 8
PEER_N_KEYS = 128
PEER_D_KEY = 256
PEER_HALF = 128
PEER_TOPK = 16
RMS_EPS = 1e-6
LANES = 128
HG_COLS = HG_HEADS * (2 * HG_DK + 2 * HG_DV)
NT_DIMS = (((1,), (1,)), ((), ()))


def _split_hi_lo(a):
    hi = a.astype(jnp.bfloat16)
    return hi, (a - hi.astype(jnp.float32)).astype(jnp.bfloat16)


PROJ_TM = 512
PROJ_VMEM = 48 * 1024 * 1024


def _in_proj_kernel(x_ref, nw_ref, w_ref, hg_ref, da_ref):
    x = x_ref[...]
    n = (x * lax.rsqrt(jnp.mean(x * x, axis=-1, keepdims=True) + RMS_EPS) * nw_ref[...]).astype(jnp.bfloat16)
    hg_ref[...] = jnp.dot(n, w_ref[:, :HG_COLS], preferred_element_type=jnp.float32)
    da_ref[...] = jnp.dot(n, w_ref[:, HG_COLS:], preferred_element_type=jnp.float32).astype(da_ref.dtype)


def _in_proj(x, nw, w):
    n, d = x.shape
    e = w.shape[1]
    tm = min(PROJ_TM, n)
    return pl.pallas_call(
        _in_proj_kernel,
        grid=(n // tm,),
        in_specs=[pl.BlockSpec((tm, d), lambda i: (i, 0)),
                  pl.BlockSpec((1, d), lambda i: (0, 0)),
                  pl.BlockSpec((d, e), lambda i: (0, 0))],
        out_specs=[pl.BlockSpec((tm, HG_COLS), lambda i: (i, 0)),
                   pl.BlockSpec((tm, e - HG_COLS), lambda i: (i, 0))],
        out_shape=[jax.ShapeDtypeStruct((n, HG_COLS), jnp.float32),
                   jax.ShapeDtypeStruct((n, e - HG_COLS), jnp.bfloat16)],
        compiler_params=pltpu.CompilerParams(dimension_semantics=("arbitrary",), vmem_limit_bytes=PROJ_VMEM),
        name="in_proj",
    )(x, nw.reshape(1, d), w)


SLAB = D_MODEL // 2 // LANES
HALF_D = D_MODEL // 2
SUBLANES = 8
PAD_EXPERTS = 1024
ROWS = 2 * SLAB


def _out_proj_kernel(hg_ref, da_ref, w_ref, x_ref, nw_ref, h_ref, xt_ref, nb_ref):
    tm = x_ref.shape[0]
    half = hg_ref.shape[1]
    h = (x_ref[...]
         + jnp.dot(hg_ref[...], w_ref[pl.ds(0, half), :], preferred_element_type=jnp.float32)
         + jnp.dot(da_ref[...], w_ref[pl.ds(half, half), :], preferred_element_type=jnp.float32))
    h_ref[...] = h
    n = h * lax.rsqrt(jnp.mean(h * h, axis=-1, keepdims=True) + RMS_EPS) * nw_ref[...]
    nb_ref[...] = n.astype(jnp.bfloat16)
    for c in range(ROWS):
        r = 2 * (c % SLAB) + c // SLAB
        xt_ref[pl.ds(r, tm, stride=ROWS), :] = n[:, c * LANES:(c + 1) * LANES]


def _out_proj(hg, da, w, x, nw):
    n, d = x.shape
    half = hg.shape[1]
    tm = min(PROJ_TM, n)
    row = lambda cols: pl.BlockSpec((tm, cols), lambda i: (i, 0))
    return pl.pallas_call(
        _out_proj_kernel,
        grid=(n // tm,),
        in_specs=[row(half), row(half), pl.BlockSpec((2 * half, d), lambda i: (0, 0)), row(d),
                  pl.BlockSpec((1, d), lambda i: (0, 0))],
        out_specs=[row(d), pl.BlockSpec((tm * ROWS, LANES), lambda i: (i, 0)), row(d)],
        out_shape=[jax.ShapeDtypeStruct((n, d), jnp.float32),
                   jax.ShapeDtypeStruct((n * ROWS, LANES), jnp.float32),
                   jax.ShapeDtypeStruct((n, d), jnp.bfloat16)],
        compiler_params=pltpu.CompilerParams(dimension_semantics=("arbitrary",), vmem_limit_bytes=PROJ_VMEM),
        name="out_proj",
    )(hg, da, w, x, nw.reshape(1, d))


SUB = 16
HG_SEQ_BLOCK = 512


def _hgrn2_kernel(q_ref, f_ref, v_ref, g_ref, lb_ref, nw_ref, o_ref, state_t):
    @pl.when(pl.program_id(1) == 0)
    def _():
        state_t[...] = jnp.zeros_like(state_t)

    lp = lb_ref[...]
    le = jnp.exp(lp - jnp.max(lp, axis=0, keepdims=True))
    lb_all = le[0:1] / jnp.sum(le, axis=0, keepdims=True)
    nw = nw_ref[...]
    r64 = lax.broadcasted_iota(jnp.int32, (CHUNK, CHUNK), 0)
    c64 = lax.broadcasted_iota(jnp.int32, (CHUNK, CHUNK), 1)
    tri = (c64 <= r64).astype(jnp.bfloat16)
    row_k = lax.broadcasted_iota(jnp.int32, (CHUNK, HG_DK), 0)
    row_s = lax.broadcasted_iota(jnp.int32, (SUB, HG_DK), 0)
    lane_s = lax.broadcasted_iota(jnp.int32, (SUB, CHUNK), 1)

    def head_chunk(hd, r0):
        cols = pl.ds(hd * HG_DK, HG_DK)
        lb = lb_all[:, hd * HG_DK:(hd + 1) * HG_DK]
        f = lb + (1.0 - lb) * jax.nn.sigmoid(f_ref[pl.ds(r0, CHUNK), cols])
        k = 1.0 - f
        hi, lo = _split_hi_lo(jnp.log(f))
        b = (jnp.dot(tri, hi, preferred_element_type=jnp.float32)
             + jnp.dot(tri, lo, preferred_element_type=jnp.float32))
        q = q_ref[pl.ds(r0, CHUNK), cols] * (HG_DK ** -0.5)
        v = v_ref[pl.ds(r0, CHUNK), cols]
        strips = []
        for i in range(CHUNK // SUB):
            lo_r = i * SUB
            bi, qi, ki = b[lo_r:lo_r + SUB], q[lo_r:lo_r + SUB], k[lo_r:lo_r + SUB]
            diag = jnp.zeros((SUB, CHUNK), jnp.float32)
            for s in range(SUB):
                dec = jnp.exp(jnp.where(row_s >= s, bi - bi[s:s + 1], -jnp.inf))
                col = jnp.sum(qi * ki[s:s + 1] * dec, axis=-1, keepdims=True)
                diag = jnp.where(lane_s == lo_r + s, col, diag)
            if i == 0:
                strips.append(diag)
            else:
                ref = b[lo_r - 1:lo_r]
                qt = (qi * jnp.exp(bi - ref)).astype(jnp.bfloat16)
                kt = (k * jnp.exp(jnp.where(row_k < lo_r, ref - b, -jnp.inf))).astype(jnp.bfloat16)
                strips.append(diag + lax.dot_general(qt, kt, NT_DIMS, preferred_element_type=jnp.float32))
        a = jnp.concatenate(strips, axis=0).astype(jnp.bfloat16)
        st = state_t[hd]
        o = jnp.dot(a, v.astype(jnp.bfloat16), preferred_element_type=jnp.float32)
        o = o + lax.dot_general((q * jnp.exp(b)).astype(jnp.bfloat16), st.astype(jnp.bfloat16), NT_DIMS,
                                preferred_element_type=jnp.float32)
        b_last = b[CHUNK - 1:CHUNK]
        kd = (k * jnp.exp(b_last - b)).astype(jnp.bfloat16)
        state_t[hd] = st * jnp.exp(b_last) + jnp.dot(v.T.astype(jnp.bfloat16), kd,
                                                     preferred_element_type=jnp.float32)
        y = o * lax.rsqrt(jnp.mean(o * o, axis=-1, keepdims=True) + RMS_EPS) * nw
        o_ref[pl.ds(r0, CHUNK), cols] = (y * jax.nn.silu(g_ref[pl.ds(r0, CHUNK), cols])).astype(o_ref.dtype)

    def chunk(ci, carry):
        r0 = pl.multiple_of(ci * CHUNK, CHUNK)
        for hd in range(HG_HEADS):
            head_chunk(hd, r0)
        return carry

    lax.fori_loop(0, q_ref.shape[0] // CHUNK, chunk, 0)


def _hgrn2_pallas(proj, lb_param, norm_w):
    bsz, seq, _ = proj.shape
    sb = min(HG_SEQ_BLOCK, seq)
    nl = lb_param.shape[0]
    width = HG_HEADS * HG_DK
    col = lambda group: pl.BlockSpec((None, sb, width), lambda b, s: (b, s, group))
    return pl.pallas_call(
        _hgrn2_kernel,
        grid=(bsz, seq // sb),
        in_specs=[col(0), col(1), col(2), col(3),
                  pl.BlockSpec((nl, width), lambda b, s: (0, 0)),
                  pl.BlockSpec((1, HG_DV), lambda b, s: (0, 0))],
        out_specs=pl.BlockSpec((None, sb, width), lambda b, s: (b, s, 0)),
        out_shape=jax.ShapeDtypeStruct((bsz, seq, width), jnp.bfloat16),
        scratch_shapes=[pltpu.VMEM((HG_HEADS, HG_DV, HG_DK), jnp.float32)],
        compiler_params=pltpu.CompilerParams(dimension_semantics=("arbitrary", "arbitrary")),
        name="hgrn2",
    )(proj, proj, proj, proj, lb_param, norm_w.reshape(1, HG_DV))


DA_TILE = 512
LAMBDA_INIT = 0.8 - 0.6 * math.exp(-0.3 * 0)


def _diff_attn_kernel(q_ref, k_ref, v_ref, lp_ref, nw_ref, o_ref, m_scr, l_scr, acc_scr):
    t = q_ref.shape[0]
    qi = pl.program_id(2)
    lane = lax.broadcasted_iota(jnp.int32, (t, 2 * DA_DH), 1)
    q = q_ref[...] * jnp.asarray(DA_DH ** -0.5, jnp.bfloat16)
    qs = [jnp.where(lane < DA_DH, q, jnp.zeros_like(q)), jnp.where(lane >= DA_DH, q, jnp.zeros_like(q))]
    m_scr[...] = jnp.full(m_scr.shape, -jnp.inf, jnp.float32)
    l_scr[...] = jnp.zeros_like(l_scr)
    acc_scr[...] = jnp.zeros_like(acc_scr)

    def sweep(kj, mask):
        k0 = pl.multiple_of(kj * t, t)
        kt = k_ref[pl.ds(k0, t), :]
        vt = v_ref[pl.ds(k0, t), :]
        for c in range(2):
            s = lax.dot_general(qs[c], kt, NT_DIMS, preferred_element_type=jnp.float32)
            if mask is not None:
                s = jnp.where(mask, s, -jnp.inf)
            m_old = m_scr[c]
            m_new = jnp.maximum(m_old, jnp.max(s, axis=-1, keepdims=True))
            alpha = jnp.exp(m_old - m_new)
            p = jnp.exp(s - jnp.tile(m_new, (1, t // LANES)))
            l_scr[c] = alpha * l_scr[c] + jnp.sum(p, axis=-1, keepdims=True)
            acc_scr[c] = alpha * acc_scr[c] + jnp.dot(p.astype(jnp.bfloat16), vt,
                                                      preferred_element_type=jnp.float32)
            m_scr[c] = m_new

    def full_tile(kj, carry):
        sweep(kj, None)
        return carry

    lax.fori_loop(0, qi, full_tile, 0)
    row = lax.broadcasted_iota(jnp.int32, (t, t), 0)
    col = lax.broadcasted_iota(jnp.int32, (t, t), 1)
    sweep(qi, (col // CHUNK) <= (row // CHUNK))

    lp = lp_ref[...]
    lam = (jnp.exp(jnp.sum(lp[0:1] * lp[1:2], axis=-1, keepdims=True))
           - jnp.exp(jnp.sum(lp[2:3] * lp[3:4], axis=-1, keepdims=True)) + LAMBDA_INIT)
    o = acc_scr[0] / l_scr[0] - lam * (acc_scr[1] / l_scr[1])
    y = o * lax.rsqrt(jnp.mean(o * o, axis=-1, keepdims=True) + RMS_EPS) * nw_ref[...]
    o_ref[...] = (y * (1.0 - LAMBDA_INIT)).astype(o_ref.dtype)


def _diff_attn_pallas(q, k, v, lam_params, norm_w, cols=(0, 0, 0)):
    bsz, seq, _ = q.shape
    t = min(DA_TILE, seq)
    cq, ck, cv = cols
    return pl.pallas_call(
        _diff_attn_kernel,
        grid=(bsz, DA_HEADS, seq // t),
        in_specs=[pl.BlockSpec((None, t, 2 * DA_DH), lambda b, h, i: (b, i, cq + h)),
                  pl.BlockSpec((None, seq, 2 * DA_DH), lambda b, h, i: (b, 0, ck + h)),
                  pl.BlockSpec((None, seq, DA_DV), lambda b, h, i: (b, 0, cv + h)),
                  pl.BlockSpec((4, DA_DH), lambda b, h, i: (0, 0)),
                  pl.BlockSpec((1, DA_DV), lambda b, h, i: (0, 0))],
        out_specs=pl.BlockSpec((None, t, DA_DV), lambda b, h, i: (b, i, h)),
        out_shape=jax.ShapeDtypeStruct((bsz, seq, DA_HEADS * DA_DV), jnp.bfloat16),
        scratch_shapes=[pltpu.VMEM((2, t, LANES), jnp.float32), pltpu.VMEM((2, t, LANES), jnp.float32),
                        pltpu.VMEM((2, t, DA_DV), jnp.float32)],
        compiler_params=pltpu.CompilerParams(dimension_semantics=("arbitrary", "arbitrary", "arbitrary")),
        name="diff_attn",
    )(q, k, v, lam_params, norm_w.reshape(1, DA_DV))


PEER_SEL = PEER_HEADS * PEER_TOPK
GATHER_COLS = PEER_SEL * ROWS
_CAND_NB = [PEER_TOPK // (a + 1) for a in range(PEER_TOPK)]
_CAND_ROWS = 56
TOPK_TB = SUBLANES * LANES


def _topk_rows(s, k):
    n = s.shape[0]
    iota = lax.broadcasted_iota(jnp.int32, s.shape, 0)
    vals, idxs = [], []
    for _ in range(k):
        m = jnp.max(s, axis=0, keepdims=True)
        idx = jnp.min(jnp.where(s == m, iota, n), axis=0, keepdims=True)
        s = jnp.where(iota == idx, -jnp.inf, s)
        vals.append(m)
        idxs.append(idx)
    return jnp.concatenate(vals, axis=0), jnp.concatenate(idxs, axis=0)


def _select_exact(s1, s2):
    v1, i1 = _topk_rows(s1, PEER_TOPK)
    v2, i2 = _topk_rows(s2, PEER_TOPK)
    tb = v1.shape[1]
    cand = jnp.concatenate(
        [v1[a:a + 1] + v2[:_CAND_NB[a]] for a in range(PEER_TOPK)]
        + [jnp.full((_CAND_ROWS - sum(_CAND_NB), tb), -jnp.inf, jnp.float32)], axis=0)
    cidx = jnp.concatenate(
        [i1[a:a + 1] * PEER_N_KEYS + i2[:_CAND_NB[a]] for a in range(PEER_TOPK)]
        + [jnp.zeros((_CAND_ROWS - sum(_CAND_NB), tb), jnp.int32)], axis=0)
    ts, pos = _topk_rows(cand, PEER_TOPK)
    iota = lax.broadcasted_iota(jnp.int32, cand.shape, 0)
    eid = jnp.concatenate(
        [jnp.sum(jnp.where(iota == pos[r:r + 1], cidx, 0), axis=0, keepdims=True)
         for r in range(PEER_TOPK)], axis=0)
    e = jnp.exp(ts - ts[0:1])
    return eid, e / jnp.sum(e, axis=0, keepdims=True)


def _sort16_pairs():
    n, pairs, p = 16, [], 1
    while p < n:
        k = p
        while k >= 1:
            for j in range(k % p, n - k, 2 * k):
                for i in range(min(k, n - j - k)):
                    if (i + j) // (2 * p) == (i + j + k) // (2 * p):
                        pairs.append((i + j, i + j + k))
            k //= 2
        p *= 2
    return pairs


_SORT16 = _sort16_pairs()


def _exchange(a, b):
    (va, ia), (vb, ib) = a, b
    swap = vb > va
    return (jnp.maximum(va, vb), jnp.where(swap, ib, ia)), (jnp.minimum(va, vb), jnp.where(swap, ia, ib))


def _top16_sorted(items):
    groups = []
    for g0 in range(0, len(items), 16):
        grp = list(items[g0:g0 + 16])
        for i, j in _SORT16:
            grp[i], grp[j] = _exchange(grp[i], grp[j])
        groups.append(grp)
    dropped = None
    while len(groups) > 1:
        merged = []
        for a, b in zip(groups[0::2], groups[1::2]):
            top = []
            for i in range(16):
                hi, lo = _exchange(a[i], b[15 - i])
                top.append(hi)
                dropped = lo[0] if dropped is None else jnp.maximum(dropped, lo[0])
            d = 8
            while d >= 1:
                for i in range(16):
                    if (i & d) == 0:
                        top[i], top[i + d] = _exchange(top[i], top[i + d])
                d //= 2
            merged.append(top)
        groups = merged
    return groups[0], dropped


def _has_tie(top, dropped):
    tie = top[PEER_TOPK - 1][0] == dropped
    for i in range(PEER_TOPK - 1):
        tie = tie | (top[i][0] == top[i + 1][0])
    return tie


def _peer_topk_kernel(n2_ref, wqt_ref, sk_ref, idx_ref, g_ref, qh_scr, s1_scr, s2_scr, k1_scr, k2_scr,
                      eid_scr, gate_scr):
    qh_scr[...] = lax.dot_general(wqt_ref[...], n2_ref[...], NT_DIMS, preferred_element_type=jnp.float32)
    groups = n2_ref.shape[0] // LANES

    def key_tiles(k_scr, n):
        return [(k_scr[pl.ds(SUBLANES * k, SUBLANES), :], jnp.full((SUBLANES, LANES), k, jnp.int32)) for k in range(n)]

    def head(h, carry):
        base = pl.multiple_of(h * PEER_D_KEY, PEER_D_KEY)
        q1 = qh_scr[pl.ds(base, PEER_HALF), :].astype(jnp.bfloat16)
        q2 = qh_scr[pl.ds(base + PEER_HALF, PEER_HALF), :].astype(jnp.bfloat16)
        s1_scr[...] = jnp.dot(sk_ref[h, 0], q1, preferred_element_type=jnp.float32)
        s2_scr[...] = jnp.dot(sk_ref[h, 1], q2, preferred_element_type=jnp.float32)
        for j in range(groups):
            k1_scr[pl.ds(j, PEER_N_KEYS, stride=SUBLANES), :] = s1_scr[:, j * LANES:(j + 1) * LANES]
            k2_scr[pl.ds(j, PEER_N_KEYS, stride=SUBLANES), :] = s2_scr[:, j * LANES:(j + 1) * LANES]

        top1, drop1 = _top16_sorted(key_tiles(k1_scr, PEER_N_KEYS))
        top2, drop2 = _top16_sorted(key_tiles(k2_scr, PEER_N_KEYS))
        cands = [(top1[a][0] + top2[b][0], top1[a][1] * PEER_N_KEYS + top2[b][1])
                 for a in range(PEER_TOPK) for b in range(_CAND_NB[a])]
        pad = (jnp.full((SUBLANES, LANES), -jnp.inf, jnp.float32), jnp.zeros((SUBLANES, LANES), jnp.int32))
        cands += [pad] * (64 - len(cands))
        top, dropc = _top16_sorted(cands)
        es = [jnp.exp(t[0] - top[0][0]) for t in top]
        den = es[0]
        for e in es[1:]:
            den = den + e
        slot0 = h * PEER_TOPK
        for r in range(PEER_TOPK):
            row = pl.multiple_of((slot0 + r) * SUBLANES, SUBLANES)
            eid_scr[pl.ds(row, SUBLANES), :] = top[r][1]
            gate_scr[pl.ds(row, SUBLANES), :] = es[r] / den

        tie = _has_tie(top1, drop1) | _has_tie(top2, drop2) | _has_tie(top, dropc)

        @pl.when(jnp.any(tie))
        def _():
            eid, gate = _select_exact(s1_scr[...], s2_scr[...])
            for r in range(PEER_TOPK):
                row = pl.multiple_of((slot0 + r) * SUBLANES, SUBLANES)
                eid_scr[pl.ds(row, SUBLANES), :] = jnp.concatenate(
                    [eid[r:r + 1, j * LANES:(j + 1) * LANES] for j in range(groups)], axis=0)
                gate_scr[pl.ds(row, SUBLANES), :] = jnp.concatenate(
                    [gate[r:r + 1, j * LANES:(j + 1) * LANES] for j in range(groups)], axis=0)

        return carry

    lax.fori_loop(0, PEER_HEADS, head, 0)
    for j in range(groups):
        eid = eid_scr[pl.ds(j, PEER_SEL, stride=SUBLANES), :]
        idx_ref[pl.ds(j * LANES, LANES), :] = ((eid + PAD_EXPERTS) * SLAB).T
        g_ref[pl.ds(j * LANES, LANES), :] = gate_scr[pl.ds(j, PEER_SEL, stride=SUBLANES), :].T


def _peer_topk(n2b, wqt, sk):
    n = n2b.shape[0]
    tb = min(TOPK_TB, n)
    return pl.pallas_call(
        _peer_topk_kernel,
        grid=(n // tb,),
        in_specs=[pl.BlockSpec((tb, D_MODEL), lambda i: (i, 0)),
                  pl.BlockSpec(wqt.shape, lambda i: (0, 0)),
                  pl.BlockSpec(sk.shape, lambda i: (0, 0, 0, 0))],
        out_specs=[pl.BlockSpec((tb, PEER_SEL), lambda i: (i, 0)),
                   pl.BlockSpec((tb, PEER_SEL), lambda i: (i, 0))],
        out_shape=[jax.ShapeDtypeStruct((n, PEER_SEL), jnp.int32),
                   jax.ShapeDtypeStruct((n, PEER_SEL), jnp.float32)],
        scratch_shapes=[pltpu.VMEM((PEER_HEADS * PEER_D_KEY, tb), jnp.float32),
                        pltpu.VMEM((PEER_N_KEYS, tb), jnp.float32),
                        pltpu.VMEM((PEER_N_KEYS, tb), jnp.float32),
                        pltpu.VMEM((PEER_N_KEYS * SUBLANES, LANES), jnp.float32),
                        pltpu.VMEM((PEER_N_KEYS * SUBLANES, LANES), jnp.float32),
                        pltpu.VMEM((PEER_SEL * SUBLANES, LANES), jnp.int32),
                        pltpu.VMEM((PEER_SEL * SUBLANES, LANES), jnp.float32)],
        compiler_params=pltpu.CompilerParams(dimension_semantics=("arbitrary",),
                                             vmem_limit_bytes=48 * 1024 * 1024),
        name="peer_topk",
    )(n2b, wqt, sk)


def _pack_kernel(t_ref, o_ref):
    i = pl.program_id(0)
    edge = (i == 0) | (i == pl.num_programs(0) - 1)

    @pl.when(edge)
    def _():
        o_ref[...] = jnp.zeros_like(o_ref)

    @pl.when(jnp.logical_not(edge))
    def _():
        rounded = t_ref[...].astype(jnp.bfloat16).astype(jnp.float32)
        bits = pltpu.bitcast(rounded, jnp.uint32)
        for j in range(SLAB):
            lo = bits[:, j * LANES:(j + 1) * LANES] >> 16
            hi = bits[:, HALF_D + j * LANES:HALF_D + (j + 1) * LANES]
            o_ref[pl.ds(j, PAD_EXPERTS, stride=SLAB), :] = pltpu.bitcast(lo | hi, jnp.int32)


def _pack_table(t):
    e, d = t.shape
    blocks = e // PAD_EXPERTS
    return pl.pallas_call(
        _pack_kernel,
        grid=(blocks + 2,),
        in_specs=[pl.BlockSpec((PAD_EXPERTS, d), lambda i: (jnp.clip(i - 1, 0, blocks - 1), 0))],
        out_specs=pl.BlockSpec((PAD_EXPERTS * SLAB, LANES), lambda i: (i, 0)),
        out_shape=jax.ShapeDtypeStruct(((e + 2 * PAD_EXPERTS) * SLAB, LANES), jnp.int32),
        compiler_params=pltpu.CompilerParams(dimension_semantics=("arbitrary",)),
        name="pack_table",
    )(t)


IDX_TOKENS = 64
GROUP = 8
PEER_VMEM = 56 * 1024 * 1024


def _index_copy(idx_hbm, ibuf, sem, blk, slot):
    return pltpu.make_async_copy(idx_hbm.at[blk], ibuf.at[slot], sem.at[slot])


def _for_both_index_buffers(idx_hbm, ibuf, sem, process):
    i = pl.program_id(0)

    @pl.when(i == 0)
    def _():
        _index_copy(idx_hbm, ibuf, sem, 0, 0).start()

    _index_copy(idx_hbm, ibuf, sem, 2 * i + 1, 1).start()
    _index_copy(idx_hbm, ibuf, sem, 2 * i, 0).wait()
    process(0)

    @pl.when(i + 1 < pl.num_programs(0))
    def _():
        _index_copy(idx_hbm, ibuf, sem, 2 * i + 2, 0).start()

    _index_copy(idx_hbm, ibuf, sem, 2 * i + 1, 1).wait()
    process(1)


def _gather_matrix(ibuf, slot, base, tab_ref):
    sub = lax.broadcasted_iota(jnp.int32, (2 * SLAB, LANES), 0)
    pairs = []
    for p in range(PEER_SEL // 2):
        ia = pl.multiple_of(ibuf[slot, base + 2 * p], SLAB)
        ib = pl.multiple_of(ibuf[slot, base + 2 * p + 1], SLAB)
        a = tab_ref[pl.ds(ia, 2 * SLAB), :]
        b = tab_ref[pl.ds(ib - SLAB, 2 * SLAB), :]
        pairs.append(jnp.where(sub < SLAB, a, b))
    return pltpu.bitcast(jnp.concatenate(pairs, axis=0), jnp.bfloat16)


def _own_row_mask():
    r = lax.broadcasted_iota(jnp.int32, (ROWS, GATHER_COLS), 0)
    c = lax.broadcasted_iota(jnp.int32, (ROWS, GATHER_COLS), 1)
    return (c % ROWS) == r


def _peer_u_kernel(idx_hbm, tab_ref, xt_ref, g_ref, w_ref, ibuf, sem, p_scr):
    own = _own_row_mask()

    def process(slot):
        for gi in range(IDX_TOKENS // GROUP):
            r0 = slot * IDX_TOKENS + gi * GROUP
            parts = []
            for k in range(GROUP):
                t = _gather_matrix(ibuf, slot, (gi * GROUP + k) * PEER_SEL, tab_ref)
                x8 = xt_ref[pl.ds(ROWS * (r0 + k), ROWS), :].astype(jnp.bfloat16)
                r = lax.dot_general(x8, t, NT_DIMS, preferred_element_type=jnp.float32)
                parts.append(jnp.sum(jnp.where(own, r, 0.0), axis=0, keepdims=True))
            p_scr[pl.ds(r0, GROUP), :] = jnp.concatenate(parts, axis=0)

    _for_both_index_buffers(idx_hbm, ibuf, sem, process)

    col = lax.broadcasted_iota(jnp.int32, (GATHER_COLS, PEER_SEL), 0)
    exp = lax.broadcasted_iota(jnp.int32, (GATHER_COLS, PEER_SEL), 1)
    fold = ((col // ROWS) == exp).astype(jnp.bfloat16)
    hi, lo = _split_hi_lo(p_scr[...])
    act = (jnp.dot(hi, fold, preferred_element_type=jnp.float32)
           + jnp.dot(lo, fold, preferred_element_type=jnp.float32))
    hi, lo = _split_hi_lo(g_ref[...] * jax.nn.gelu(act))
    w_ref[...] = (lax.dot_general(hi, fold, NT_DIMS, preferred_element_type=jnp.float32)
                  + lax.dot_general(lo, fold, NT_DIMS, preferred_element_type=jnp.float32))


RES_STRIDE = 2 * IDX_TOKENS + SUBLANES


def _peer_v_kernel(idx_hbm, tab_ref, w_ref, h_ref, nw_ref, o_ref, ibuf, sem, res):
    own = _own_row_mask()
    tb = h_ref.shape[0]

    def process(slot):
        for gi in range(IDX_TOKENS // GROUP):
            r0 = slot * IDX_TOKENS + gi * GROUP
            wg = w_ref[pl.ds(r0, GROUP), :]
            for k in range(GROUP):
                t = _gather_matrix(ibuf, slot, (gi * GROUP + k) * PEER_SEL, tab_ref)
                w8 = jnp.where(own, wg[k:k + 1], 0.0).astype(jnp.bfloat16)
                res[pl.ds(r0 + k, ROWS, stride=RES_STRIDE), :] = jnp.dot(w8, t, preferred_element_type=jnp.float32)

    _for_both_index_buffers(idx_hbm, ibuf, sem, process)

    planes = [res[pl.ds((2 * (c % SLAB) + c // SLAB) * RES_STRIDE, tb), :] for c in range(ROWS)]
    y = h_ref[...] + jnp.concatenate(planes, axis=-1)
    y = y * lax.rsqrt(jnp.mean(y * y, axis=-1, keepdims=True) + RMS_EPS)
    o_ref[...] = y * nw_ref[...]


def _peer_u(idx_rows, tab, xt, g):
    tb = 2 * IDX_TOKENS
    n = g.shape[0]
    return pl.pallas_call(
        _peer_u_kernel,
        grid=(n // tb,),
        in_specs=[pl.BlockSpec(memory_space=pl.ANY),
                  pl.BlockSpec(memory_space=pltpu.VMEM),
                  pl.BlockSpec((tb * ROWS, LANES), lambda i: (i, 0)),
                  pl.BlockSpec((tb, PEER_SEL), lambda i: (i, 0))],
        out_specs=pl.BlockSpec((tb, GATHER_COLS), lambda i: (i, 0)),
        out_shape=jax.ShapeDtypeStruct((n, GATHER_COLS), jnp.float32),
        scratch_shapes=[pltpu.SMEM((2, IDX_TOKENS * PEER_SEL), jnp.int32),
                        pltpu.SemaphoreType.DMA((2,)),
                        pltpu.VMEM((tb, GATHER_COLS), jnp.float32)],
        compiler_params=pltpu.CompilerParams(dimension_semantics=("arbitrary",), vmem_limit_bytes=PEER_VMEM),
        name="peer_u",
    )(idx_rows, tab, xt, g)


def _peer_v(idx_rows, tab, w, h, nw):
    tb = 2 * IDX_TOKENS
    n = h.shape[0]
    return pl.pallas_call(
        _peer_v_kernel,
        grid=(n // tb,),
        in_specs=[pl.BlockSpec(memory_space=pl.ANY),
                  pl.BlockSpec(memory_space=pltpu.VMEM),
                  pl.BlockSpec((tb, GATHER_COLS), lambda i: (i, 0)),
                  pl.BlockSpec((tb, D_MODEL), lambda i: (i, 0)),
                  pl.BlockSpec((1, D_MODEL), lambda i: (0, 0))],
        out_specs=pl.BlockSpec((tb, D_MODEL), lambda i: (i, 0)),
        out_shape=jax.ShapeDtypeStruct((n, D_MODEL), jnp.float32),
        scratch_shapes=[pltpu.SMEM((2, IDX_TOKENS * PEER_SEL), jnp.int32),
                        pltpu.SemaphoreType.DMA((2,)),
                        pltpu.VMEM((ROWS * RES_STRIDE, LANES), jnp.float32)],
        compiler_params=pltpu.CompilerParams(dimension_semantics=("arbitrary",), vmem_limit_bytes=PEER_VMEM),
        name="peer_v",
    )(idx_rows, tab, w, h, nw)


def _peer_block(xt, n2b, h, w_query, sub_keys, u, v, norm_final_w):
    n = h.shape[0]
    idx, g = _peer_topk(n2b, w_query.T.astype(jnp.bfloat16), sub_keys.astype(jnp.bfloat16))
    idx_rows = idx.reshape(n // IDX_TOKENS, IDX_TOKENS * PEER_SEL)
    w = _peer_u(idx_rows, _pack_table(u), xt, g)
    return _peer_v(idx_rows, _pack_table(v), w, h, norm_final_w.reshape(1, D_MODEL))


def kernel(x, norm_mix_w, w_in, hgrn_lb_param, hgrn_norm_w, diff_lambda, diff_norm_w,
           w_out, norm_ffn_w, peer_w_query, peer_sub_keys, peer_u, peer_v, norm_final_w):
    bsz, seq, d = x.shape
    n = bsz * seq
    x2 = x.reshape(n, d)
    proj_hg, proj_da = _in_proj(x2, norm_mix_w[0], w_in[0].astype(jnp.bfloat16))
    hg_out = _hgrn2_pallas(proj_hg.reshape(bsz, seq, -1), hgrn_lb_param, hgrn_norm_w[0])
    proj_da = proj_da.reshape(bsz, seq, -1)
    da_out = _diff_attn_pallas(proj_da, proj_da, proj_da, diff_lambda[0], diff_norm_w[0],
                               cols=(0, DA_HEADS, 2 * DA_HEADS))
    h, xt, n2b = _out_proj(hg_out.reshape(n, -1), da_out.reshape(n, -1), w_out[0].astype(jnp.bfloat16),
                           x2, norm_ffn_w[0])
    out = _peer_block(xt, n2b, h, peer_w_query[0], peer_sub_keys[0], peer_u[0], peer_v[0], norm_final_w)
    return out.reshape(bsz, seq, d)
```

```python
import math
import jax, jax.numpy as jnp
from jax import lax
from jax.experimental import pallas as pl
from jax.experimental.pallas import tpu as pltpu

D_MODEL = 1024
CHUNK = 64
HG_HEADS = 4
HG_DK = 128
HG_DV = 128
DA_HEADS = 4
DA_DV = 128
DA_DH = 64
PEER_HEADS = 8
PEER_N_KEYS = 128
PEER_D_KEY = 256
PEER_HALF = 128
PEER_TOPK = 16
RMS_EPS = 1e-6
LANES = 128
HG_COLS = HG_HEADS * (2 * HG_DK + 2 * HG_DV)
NT_DIMS = (((1,), (1,)), ((), ()))


def _split_hi_lo(a):
    hi = a.astype(jnp.bfloat16)
    return hi, (a - hi.astype(jnp.float32)).astype(jnp.bfloat16)


PROJ_TM = 512
PROJ_VMEM = 48 * 1024 * 1024


def _in_proj_kernel(x_ref, nw_ref, w_ref, hg_ref, da_ref):
    x = x_ref[...]
    n = (x * lax.rsqrt(jnp.mean(x * x, axis=-1, keepdims=True) + RMS_EPS) * nw_ref[...]).astype(jnp.bfloat16)
    hg_ref[...] = jnp.dot(n, w_ref[:, :HG_COLS], preferred_element_type=jnp.float32)
    da_ref[...] = jnp.dot(n, w_ref[:, HG_COLS:], preferred_element_type=jnp.float32).astype(da_ref.dtype)


def _in_proj(x, nw, w):
    n, d = x.shape
    e = w.shape[1]
    tm = min(PROJ_TM, n)
    return pl.pallas_call(
        _in_proj_kernel,
        grid=(n // tm,),
        in_specs=[pl.BlockSpec((tm, d), lambda i: (i, 0)),
                  pl.BlockSpec((1, d), lambda i: (0, 0)),
                  pl.BlockSpec((d, e), lambda i: (0, 0))],
        out_specs=[pl.BlockSpec((tm, HG_COLS), lambda i: (i, 0)),
                   pl.BlockSpec((tm, e - HG_COLS), lambda i: (i, 0))],
        out_shape=[jax.ShapeDtypeStruct((n, HG_COLS), jnp.float32),
                   jax.ShapeDtypeStruct((n, e - HG_COLS), jnp.bfloat16)],
        compiler_params=pltpu.CompilerParams(dimension_semantics=("arbitrary",), vmem_limit_bytes=PROJ_VMEM),
        name="in_proj",
    )(x, nw.reshape(1, d), w)


SLAB = D_MODEL // 2 // LANES
HALF_D = D_MODEL // 2
SUBLANES = 8
PAD_EXPERTS = 1024
ROWS = 2 * SLAB


def _out_proj_kernel(hg_ref, da_ref, w_ref, x_ref, nw_ref, h_ref, xt_ref, nb_ref):
    tm = x_ref.shape[0]
    half = hg_ref.shape[1]
    h = (x_ref[...]
         + jnp.dot(hg_ref[...], w_ref[pl.ds(0, half), :], preferred_element_type=jnp.float32)
         + jnp.dot(da_ref[...], w_ref[pl.ds(half, half), :], preferred_element_type=jnp.float32))
    h_ref[...] = h
    n = h * lax.rsqrt(jnp.mean(h * h, axis=-1, keepdims=True) + RMS_EPS) * nw_ref[...]
    nb_ref[...] = n.astype(jnp.bfloat16)
    for c in range(ROWS):
        r = 2 * (c % SLAB) + c // SLAB
        xt_ref[pl.ds(r, tm, stride=ROWS), :] = n[:, c * LANES:(c + 1) * LANES]


def _out_proj(hg, da, w, x, nw):
    n, d = x.shape
    half = hg.shape[1]
    tm = min(PROJ_TM, n)
    row = lambda cols: pl.BlockSpec((tm, cols), lambda i: (i, 0))
    return pl.pallas_call(
        _out_proj_kernel,
        grid=(n // tm,),
        in_specs=[row(half), row(half), pl.BlockSpec((2 * half, d), lambda i: (0, 0)), row(d),
                  pl.BlockSpec((1, d), lambda i: (0, 0))],
        out_specs=[row(d), pl.BlockSpec((tm * ROWS, LANES), lambda i: (i, 0)), row(d)],
        out_shape=[jax.ShapeDtypeStruct((n, d), jnp.float32),
                   jax.ShapeDtypeStruct((n * ROWS, LANES), jnp.float32),
                   jax.ShapeDtypeStruct((n, d), jnp.bfloat16)],
        compiler_params=pltpu.CompilerParams(dimension_semantics=("arbitrary",), vmem_limit_bytes=PROJ_VMEM),
        name="out_proj",
    )(hg, da, w, x, nw.reshape(1, d))


SUB = 16
HG_SEQ_BLOCK = 512


def _hgrn2_kernel(q_ref, f_ref, v_ref, g_ref, lb_ref, nw_ref, o_ref, state_t):
    @pl.when(pl.program_id(1) == 0)
    def _():
        state_t[...] = jnp.zeros_like(state_t)

    lp = lb_ref[...]
    le = jnp.exp(lp - jnp.max(lp, axis=0, keepdims=True))
    lb_all = le[0:1] / jnp.sum(le, axis=0, keepdims=True)
    nw = nw_ref[...]
    r64 = lax.broadcasted_iota(jnp.int32, (CHUNK, CHUNK), 0)
    c64 = lax.broadcasted_iota(jnp.int32, (CHUNK, CHUNK), 1)
    tri = (c64 <= r64).astype(jnp.bfloat16)
    row_k = lax.broadcasted_iota(jnp.int32, (CHUNK, HG_DK), 0)
    row_s = lax.broadcasted_iota(jnp.int32, (SUB, HG_DK), 0)
    lane_s = lax.broadcasted_iota(jnp.int32, (SUB, CHUNK), 1)

    def head_chunk(hd, r0):
        cols = pl.ds(hd * HG_DK, HG_DK)
        lb = lb_all[:, hd * HG_DK:(hd + 1) * HG_DK]
        f = lb + (1.0 - lb) * jax.nn.sigmoid(f_ref[pl.ds(r0, CHUNK), cols])
        k = 1.0 - f
        hi, lo = _split_hi_lo(jnp.log(f))
        b = (jnp.dot(tri, hi, preferred_element_type=jnp.float32)
             + jnp.dot(tri, lo, preferred_element_type=jnp.float32))
        q = q_ref[pl.ds(r0, CHUNK), cols] * (HG_DK ** -0.5)
        v = v_ref[pl.ds(r0, CHUNK), cols]
        strips = []
        for i in range(CHUNK // SUB):
            lo_r = i * SUB
            bi, qi, ki = b[lo_r:lo_r + SUB], q[lo_r:lo_r + SUB], k[lo_r:lo_r + SUB]
            diag = jnp.zeros((SUB, CHUNK), jnp.float32)
            for s in range(SUB):
                dec = jnp.exp(jnp.where(row_s >= s, bi - bi[s:s + 1], -jnp.inf))
                col = jnp.sum(qi * ki[s:s + 1] * dec, axis=-1, keepdims=True)
                diag = jnp.where(lane_s == lo_r + s, col, diag)
            if i == 0:
                strips.append(diag)
            else:
                ref = b[lo_r - 1:lo_r]
                qt = (qi * jnp.exp(bi - ref)).astype(jnp.bfloat16)
                kt = (k * jnp.exp(jnp.where(row_k < lo_r, ref - b, -jnp.inf))).astype(jnp.bfloat16)
                strips.append(diag + lax.dot_general(qt, kt, NT_DIMS, preferred_element_type=jnp.float32))
        a = jnp.concatenate(strips, axis=0).astype(jnp.bfloat16)
        st = state_t[hd]
        o = jnp.dot(a, v.astype(jnp.bfloat16), preferred_element_type=jnp.float32)
        o = o + lax.dot_general((q * jnp.exp(b)).astype(jnp.bfloat16), st.astype(jnp.bfloat16), NT_DIMS,
                                preferred_element_type=jnp.float32)
        b_last = b[CHUNK - 1:CHUNK]
        kd = (k * jnp.exp(b_last - b)).astype(jnp.bfloat16)
        state_t[hd] = st * jnp.exp(b_last) + jnp.dot(v.T.astype(jnp.bfloat16), kd,
                                                     preferred_element_type=jnp.float32)
        y = o * lax.rsqrt(jnp.mean(o * o, axis=-1, keepdims=True) + RMS_EPS) * nw
        o_ref[pl.ds(r0, CHUNK), cols] = (y * jax.nn.silu(g_ref[pl.ds(r0, CHUNK), cols])).astype(o_ref.dtype)

    def chunk(ci, carry):
        r0 = pl.multiple_of(ci * CHUNK, CHUNK)
        for hd in range(HG_HEADS):
            head_chunk(hd, r0)
        return carry

    lax.fori_loop(0, q_ref.shape[0] // CHUNK, chunk, 0)


def _hgrn2_pallas(proj, lb_param, norm_w):
    bsz, seq, _ = proj.shape
    sb = min(HG_SEQ_BLOCK, seq)
    nl = lb_param.shape[0]
    width = HG_HEADS * HG_DK
    col = lambda group: pl.BlockSpec((None, sb, width), lambda b, s: (b, s, group))
    return pl.pallas_call(
        _hgrn2_kernel,
        grid=(bsz, seq // sb),
        in_specs=[col(0), col(1), col(2), col(3),
                  pl.BlockSpec((nl, width), lambda b, s: (0, 0)),
                  pl.BlockSpec((1, HG_DV), lambda b, s: (0, 0))],
        out_specs=pl.BlockSpec((None, sb, width), lambda b, s: (b, s, 0)),
        out_shape=jax.ShapeDtypeStruct((bsz, seq, width), jnp.bfloat16),
        scratch_shapes=[pltpu.VMEM((HG_HEADS, HG_DV, HG_DK), jnp.float32)],
        compiler_params=pltpu.CompilerParams(dimension_semantics=("arbitrary", "arbitrary")),
        name="hgrn2",
    )(proj, proj, proj, proj, lb_param, norm_w.reshape(1, HG_DV))


DA_TILE = 1024
DA_KEY_TILE = 512
LAMBDA_INIT = 0.8 - 0.6 * math.exp(-0.3 * 0)


def _diff_attn_kernel(q_ref, k_ref, v_ref, lp_ref, nw_ref, o_ref, m_scr, l_scr, acc_scr):
    t = q_ref.shape[0]
    tk = min(DA_KEY_TILE, t)
    qi = pl.program_id(2)
    lane = lax.broadcasted_iota(jnp.int32, (t, 2 * DA_DH), 1)
    q = q_ref[...] * jnp.asarray(DA_DH ** -0.5, jnp.bfloat16)
    qs = [jnp.where(lane < DA_DH, q, jnp.zeros_like(q)), jnp.where(lane >= DA_DH, q, jnp.zeros_like(q))]
    m_scr[...] = jnp.full(m_scr.shape, -jnp.inf, jnp.float32)
    l_scr[...] = jnp.zeros_like(l_scr)
    acc_scr[...] = jnp.zeros_like(acc_scr)

    def sweep(kj, r0, masked):
        k0 = pl.multiple_of(kj * tk, tk)
        kt = k_ref[pl.ds(k0, tk), :]
        vt = v_ref[pl.ds(k0, tk), :]
        rows = pl.ds(r0, t - r0)
        for c in range(2):
            s = lax.dot_general(qs[c][r0:], kt, NT_DIMS, preferred_element_type=jnp.float32)
            if masked:
                row = lax.broadcasted_iota(jnp.int32, s.shape, 0)
                col = lax.broadcasted_iota(jnp.int32, s.shape, 1)
                s = jnp.where((col // CHUNK) <= (row // CHUNK), s, -jnp.inf)
            m_old = m_scr[c, rows, :]
            m_new = jnp.maximum(m_old, jnp.max(s, axis=-1, keepdims=True))
            alpha = jnp.exp(m_old - m_new)
            p = jnp.exp(s - jnp.tile(m_new, (1, tk // LANES)))
            l_scr[c, rows, :] = alpha * l_scr[c, rows, :] + jnp.sum(p, axis=-1, keepdims=True)
            acc_scr[c, rows, :] = alpha * acc_scr[c, rows, :] + jnp.dot(p.astype(jnp.bfloat16), vt,
                                                                        preferred_element_type=jnp.float32)
            m_scr[c, rows, :] = m_new

    def full_tile(kj, carry):
        sweep(kj, 0, False)
        return carry

    per_q = t // tk
    lax.fori_loop(0, per_q * qi, full_tile, 0)
    for d in range(per_q):
        sweep(per_q * qi + d, d * tk, True)

    lp = lp_ref[...]
    lam = (jnp.exp(jnp.sum(lp[0:1] * lp[1:2], axis=-1, keepdims=True))
           - jnp.exp(jnp.sum(lp[2:3] * lp[3:4], axis=-1, keepdims=True)) + LAMBDA_INIT)
    o = acc_scr[0] / l_scr[0] - lam * (acc_scr[1] / l_scr[1])
    y = o * lax.rsqrt(jnp.mean(o * o, axis=-1, keepdims=True) + RMS_EPS) * nw_ref[...]
    o_ref[...] = (y * (1.0 - LAMBDA_INIT)).astype(o_ref.dtype)


def _diff_attn_pallas(q, k, v, lam_params, norm_w, cols=(0, 0, 0)):
    bsz, seq, _ = q.shape
    t = min(DA_TILE, seq)
    cq, ck, cv = cols
    return pl.pallas_call(
        _diff_attn_kernel,
        grid=(bsz, DA_HEADS, seq // t),
        in_specs=[pl.BlockSpec((None, t, 2 * DA_DH), lambda b, h, i: (b, i, cq + h)),
                  pl.BlockSpec((None, seq, 2 * DA_DH), lambda b, h, i: (b, 0, ck + h)),
                  pl.BlockSpec((None, seq, DA_DV), lambda b, h, i: (b, 0, cv + h)),
                  pl.BlockSpec((4, DA_DH), lambda b, h, i: (0, 0)),
                  pl.BlockSpec((1, DA_DV), lambda b, h, i: (0, 0))],
        out_specs=pl.BlockSpec((None, t, DA_DV), lambda b, h, i: (b, i, h)),
        out_shape=jax.ShapeDtypeStruct((bsz, seq, DA_HEADS * DA_DV), jnp.bfloat16),
        scratch_shapes=[pltpu.VMEM((2, t, LANES), jnp.float32), pltpu.VMEM((2, t, LANES), jnp.float32),
                        pltpu.VMEM((2, t, DA_DV), jnp.float32)],
        compiler_params=pltpu.CompilerParams(dimension_semantics=("arbitrary", "arbitrary", "arbitrary")),
        name="diff_attn",
    )(q, k, v, lam_params, norm_w.reshape(1, DA_DV))


PEER_SEL = PEER_HEADS * PEER_TOPK
GATHER_COLS = PEER_SEL * ROWS
_CAND_NB = [PEER_TOPK // (a + 1) for a in range(PEER_TOPK)]
_CAND_ROWS = 56
TOPK_TB = SUBLANES * LANES


def _topk_rows(s, k):
    n = s.shape[0]
    iota = lax.broadcasted_iota(jnp.int32, s.shape, 0)
    vals, idxs = [], []
    for _ in range(k):
        m = jnp.max(s, axis=0, keepdims=True)
        idx = jnp.min(jnp.where(s == m, iota, n), axis=0, keepdims=True)
        s = jnp.where(iota == idx, -jnp.inf, s)
        vals.append(m)
        idxs.append(idx)
    return jnp.concatenate(vals, axis=0), jnp.concatenate(idxs, axis=0)


def _select_exact(s1, s2):
    v1, i1 = _topk_rows(s1, PEER_TOPK)
    v2, i2 = _topk_rows(s2, PEER_TOPK)
    tb = v1.shape[1]
    cand = jnp.concatenate(
        [v1[a:a + 1] + v2[:_CAND_NB[a]] for a in range(PEER_TOPK)]
        + [jnp.full((_CAND_ROWS - sum(_CAND_NB), tb), -jnp.inf, jnp.float32)], axis=0)
    cidx = jnp.concatenate(
        [i1[a:a + 1] * PEER_N_KEYS + i2[:_CAND_NB[a]] for a in range(PEER_TOPK)]
        + [jnp.zeros((_CAND_ROWS - sum(_CAND_NB), tb), jnp.int32)], axis=0)
    ts, pos = _topk_rows(cand, PEER_TOPK)
    iota = lax.broadcasted_iota(jnp.int32, cand.shape, 0)
    eid = jnp.concatenate(
        [jnp.sum(jnp.where(iota == pos[r:r + 1], cidx, 0), axis=0, keepdims=True)
         for r in range(PEER_TOPK)], axis=0)
    e = jnp.exp(ts - ts[0:1])
    return eid, e / jnp.sum(e, axis=0, keepdims=True)


def _sort16_pairs():
    n, pairs, p = 16, [], 1
    while p < n:
        k = p
        while k >= 1:
            for j in range(k % p, n - k, 2 * k):
                for i in range(min(k, n - j - k)):
                    if (i + j) // (2 * p) == (i + j + k) // (2 * p):
                        pairs.append((i + j, i + j + k))
            k //= 2
        p *= 2
    return pairs


_SORT16 = _sort16_pairs()


def _exchange(a, b):
    (va, ia), (vb, ib) = a, b
    swap = vb > va
    return (jnp.maximum(va, vb), jnp.where(swap, ib, ia)), (jnp.minimum(va, vb), jnp.where(swap, ia, ib))


def _top16_sorted(items):
    groups = []
    for g0 in range(0, len(items), 16):
        grp = list(items[g0:g0 + 16])
        for i, j in _SORT16:
            grp[i], grp[j] = _exchange(grp[i], grp[j])
        groups.append(grp)
    dropped = None
    while len(groups) > 1:
        merged = []
        for a, b in zip(groups[0::2], groups[1::2]):
            top = []
            for i in range(16):
                hi, lo = _exchange(a[i], b[15 - i])
                top.append(hi)
                dropped = lo[0] if dropped is None else jnp.maximum(dropped, lo[0])
            d = 8
            while d >= 1:
                for i in range(16):
                    if (i & d) == 0:
                        top[i], top[i + d] = _exchange(top[i], top[i + d])
                d //= 2
            merged.append(top)
        groups = merged
    return groups[0], dropped


def _has_tie(top, dropped):
    tie = top[PEER_TOPK - 1][0] == dropped
    for i in range(PEER_TOPK - 1):
        tie = tie | (top[i][0] == top[i + 1][0])
    return tie


def _peer_topk_kernel(n2_ref, wqt_ref, sk_ref, idx_ref, g_ref, qh_scr, s1_scr, s2_scr, k1_scr, k2_scr,
                      eid_scr, gate_scr):
    qh_scr[...] = lax.dot_general(wqt_ref[...], n2_ref[...], NT_DIMS, preferred_element_type=jnp.float32)
    groups = n2_ref.shape[0] // LANES

    def key_tiles(k_scr, n):
        return [(k_scr[pl.ds(SUBLANES * k, SUBLANES), :], jnp.full((SUBLANES, LANES), k, jnp.int32)) for k in range(n)]

    def head(h, carry):
        base = pl.multiple_of(h * PEER_D_KEY, PEER_D_KEY)
        q1 = qh_scr[pl.ds(base, PEER_HALF), :].astype(jnp.bfloat16)
        q2 = qh_scr[pl.ds(base + PEER_HALF, PEER_HALF), :].astype(jnp.bfloat16)
        s1_scr[...] = jnp.dot(sk_ref[h, 0], q1, preferred_element_type=jnp.float32)
        s2_scr[...] = jnp.dot(sk_ref[h, 1], q2, preferred_element_type=jnp.float32)
        for j in range(groups):
            k1_scr[pl.ds(j, PEER_N_KEYS, stride=SUBLANES), :] = s1_scr[:, j * LANES:(j + 1) * LANES]
            k2_scr[pl.ds(j, PEER_N_KEYS, stride=SUBLANES), :] = s2_scr[:, j * LANES:(j + 1) * LANES]

        top1, drop1 = _top16_sorted(key_tiles(k1_scr, PEER_N_KEYS))
        top2, drop2 = _top16_sorted(key_tiles(k2_scr, PEER_N_KEYS))
        cands = [(top1[a][0] + top2[b][0], top1[a][1] * PEER_N_KEYS + top2[b][1])
                 for a in range(PEER_TOPK) for b in range(_CAND_NB[a])]
        pad = (jnp.full((SUBLANES, LANES), -jnp.inf, jnp.float32), jnp.zeros((SUBLANES, LANES), jnp.int32))
        cands += [pad] * (64 - len(cands))
        top, dropc = _top16_sorted(cands)
        es = [jnp.exp(t[0] - top[0][0]) for t in top]
        den = es[0]
        for e in es[1:]:
            den = den + e
        slot0 = h * PEER_TOPK
        for r in range(PEER_TOPK):
            row = pl.multiple_of((slot0 + r) * SUBLANES, SUBLANES)
            eid_scr[pl.ds(row, SUBLANES), :] = top[r][1]
            gate_scr[pl.ds(row, SUBLANES), :] = es[r] / den

        tie = _has_tie(top1, drop1) | _has_tie(top2, drop2) | _has_tie(top, dropc)

        @pl.when(jnp.any(tie))
        def _():
            eid, gate = _select_exact(s1_scr[...], s2_scr[...])
            for r in range(PEER_TOPK):
                row = pl.multiple_of((slot0 + r) * SUBLANES, SUBLANES)
                eid_scr[pl.ds(row, SUBLANES), :] = jnp.concatenate(
                    [eid[r:r + 1, j * LANES:(j + 1) * LANES] for j in range(groups)], axis=0)
                gate_scr[pl.ds(row, SUBLANES), :] = jnp.concatenate(
                    [gate[r:r + 1, j * LANES:(j + 1) * LANES] for j in range(groups)], axis=0)

        return carry

    lax.fori_loop(0, PEER_HEADS, head, 0)
    for j in range(groups):
        eid = eid_scr[pl.ds(j, PEER_SEL, stride=SUBLANES), :]
        idx_ref[pl.ds(j * LANES, LANES), :] = ((eid + PAD_EXPERTS) * SLAB).T
        g_ref[pl.ds(j * LANES, LANES), :] = gate_scr[pl.ds(j, PEER_SEL, stride=SUBLANES), :].T


def _peer_topk(n2b, wqt, sk):
    n = n2b.shape[0]
    tb = min(TOPK_TB, n)
    return pl.pallas_call(
        _peer_topk_kernel,
        grid=(n // tb,),
        in_specs=[pl.BlockSpec((tb, D_MODEL), lambda i: (i, 0)),
                  pl.BlockSpec(wqt.shape, lambda i: (0, 0)),
                  pl.BlockSpec(sk.shape, lambda i: (0, 0, 0, 0))],
        out_specs=[pl.BlockSpec((tb, PEER_SEL), lambda i: (i, 0)),
                   pl.BlockSpec((tb, PEER_SEL), lambda i: (i, 0))],
        out_shape=[jax.ShapeDtypeStruct((n, PEER_SEL), jnp.int32),
                   jax.ShapeDtypeStruct((n, PEER_SEL), jnp.float32)],
        scratch_shapes=[pltpu.VMEM((PEER_HEADS * PEER_D_KEY, tb), jnp.float32),
                        pltpu.VMEM((PEER_N_KEYS, tb), jnp.float32),
                        pltpu.VMEM((PEER_N_KEYS, tb), jnp.float32),
                        pltpu.VMEM((PEER_N_KEYS * SUBLANES, LANES), jnp.float32),
                        pltpu.VMEM((PEER_N_KEYS * SUBLANES, LANES), jnp.float32),
                        pltpu.VMEM((PEER_SEL * SUBLANES, LANES), jnp.int32),
                        pltpu.VMEM((PEER_SEL * SUBLANES, LANES), jnp.float32)],
        compiler_params=pltpu.CompilerParams(dimension_semantics=("arbitrary",),
                                             vmem_limit_bytes=48 * 1024 * 1024),
        name="peer_topk",
    )(n2b, wqt, sk)


def _pack_kernel(t_ref, o_ref):
    i = pl.program_id(0)
    edge = (i == 0) | (i == pl.num_programs(0) - 1)

    @pl.when(edge)
    def _():
        o_ref[...] = jnp.zeros_like(o_ref)

    @pl.when(jnp.logical_not(edge))
    def _():
        rounded = t_ref[...].astype(jnp.bfloat16).astype(jnp.float32)
        bits = pltpu.bitcast(rounded, jnp.uint32)
        for j in range(SLAB):
            lo = bits[:, j * LANES:(j + 1) * LANES] >> 16
            hi = bits[:, HALF_D + j * LANES:HALF_D + (j + 1) * LANES]
            o_ref[pl.ds(j, PAD_EXPERTS, stride=SLAB), :] = pltpu.bitcast(lo | hi, jnp.int32)


def _pack_table(t):
    e, d = t.shape
    blocks = e // PAD_EXPERTS
    return pl.pallas_call(
        _pack_kernel,
        grid=(blocks + 2,),
        in_specs=[pl.BlockSpec((PAD_EXPERTS, d), lambda i: (jnp.clip(i - 1, 0, blocks - 1), 0))],
        out_specs=pl.BlockSpec((PAD_EXPERTS * SLAB, LANES), lambda i: (i, 0)),
        out_shape=jax.ShapeDtypeStruct(((e + 2 * PAD_EXPERTS) * SLAB, LANES), jnp.int32),
        compiler_params=pltpu.CompilerParams(dimension_semantics=("arbitrary",)),
        name="pack_table",
    )(t)


IDX_TOKENS = 64
GROUP = 8
PEER_VMEM = 56 * 1024 * 1024


def _index_copy(idx_hbm, ibuf, sem, blk, slot):
    return pltpu.make_async_copy(idx_hbm.at[blk], ibuf.at[slot], sem.at[slot])


def _for_both_index_buffers(idx_hbm, ibuf, sem, process):
    i = pl.program_id(0)

    @pl.when(i == 0)
    def _():
        _index_copy(idx_hbm, ibuf, sem, 0, 0).start()

    _index_copy(idx_hbm, ibuf, sem, 2 * i + 1, 1).start()
    _index_copy(idx_hbm, ibuf, sem, 2 * i, 0).wait()
    process(0)

    @pl.when(i + 1 < pl.num_programs(0))
    def _():
        _index_copy(idx_hbm, ibuf, sem, 2 * i + 2, 0).start()

    _index_copy(idx_hbm, ibuf, sem, 2 * i + 1, 1).wait()
    process(1)


def _gather_matrix(ibuf, slot, base, tab_ref):
    sub = lax.broadcasted_iota(jnp.int32, (2 * SLAB, LANES), 0)
    pairs = []
    for p in range(PEER_SEL // 2):
        ia = pl.multiple_of(ibuf[slot, base + 2 * p], SLAB)
        ib = pl.multiple_of(ibuf[slot, base + 2 * p + 1], SLAB)
        a = tab_ref[pl.ds(ia, 2 * SLAB), :]
        b = tab_ref[pl.ds(ib - SLAB, 2 * SLAB), :]
        pairs.append(jnp.where(sub < SLAB, a, b))
    return pltpu.bitcast(jnp.concatenate(pairs, axis=0), jnp.bfloat16)


def _own_row_mask():
    r = lax.broadcasted_iota(jnp.int32, (ROWS, GATHER_COLS), 0)
    c = lax.broadcasted_iota(jnp.int32, (ROWS, GATHER_COLS), 1)
    return (c % ROWS) == r


def _peer_u_kernel(idx_hbm, tab_ref, xt_ref, g_ref, w_ref, ibuf, sem, p_scr):
    own = _own_row_mask()

    def process(slot):
        for gi in range(IDX_TOKENS // GROUP):
            r0 = slot * IDX_TOKENS + gi * GROUP
            parts = []
            for k in range(GROUP):
                t = _gather_matrix(ibuf, slot, (gi * GROUP + k) * PEER_SEL, tab_ref)
                x8 = xt_ref[pl.ds(ROWS * (r0 + k), ROWS), :].astype(jnp.bfloat16)
                r = lax.dot_general(x8, t, NT_DIMS, preferred_element_type=jnp.float32)
                parts.append(jnp.sum(jnp.where(own, r, 0.0), axis=0, keepdims=True))
            p_scr[pl.ds(r0, GROUP), :] = jnp.concatenate(parts, axis=0)

    _for_both_index_buffers(idx_hbm, ibuf, sem, process)

    col = lax.broadcasted_iota(jnp.int32, (GATHER_COLS, PEER_SEL), 0)
    exp = lax.broadcasted_iota(jnp.int32, (GATHER_COLS, PEER_SEL), 1)
    fold = ((col // ROWS) == exp).astype(jnp.bfloat16)
    hi, lo = _split_hi_lo(p_scr[...])
    act = (jnp.dot(hi, fold, preferred_element_type=jnp.float32)
           + jnp.dot(lo, fold, preferred_element_type=jnp.float32))
    hi, lo = _split_hi_lo(g_ref[...] * jax.nn.gelu(act))
    w_ref[...] = (lax.dot_general(hi, fold, NT_DIMS, preferred_element_type=jnp.float32)
                  + lax.dot_general(lo, fold, NT_DIMS, preferred_element_type=jnp.float32))


RES_STRIDE = 2 * IDX_TOKENS + SUBLANES


def _peer_v_kernel(idx_hbm, tab_ref, w_ref, h_ref, nw_ref, o_ref, ibuf, sem, res):
    own = _own_row_mask()
    tb = h_ref.shape[0]

    def process(slot):
        for gi in range(IDX_TOKENS // GROUP):
            r0 = slot * IDX_TOKENS + gi * GROUP
            wg = w_ref[pl.ds(r0, GROUP), :]
            for k in range(GROUP):
                t = _gather_matrix(ibuf, slot, (gi * GROUP + k) * PEER_SEL, tab_ref)
                w8 = jnp.where(own, wg[k:k + 1], 0.0).astype(jnp.bfloat16)
                res[pl.ds(r0 + k, ROWS, stride=RES_STRIDE), :] = jnp.dot(w8, t, preferred_element_type=jnp.float32)

    _for_both_index_buffers(idx_hbm, ibuf, sem, process)

    planes = [res[pl.ds((2 * (c % SLAB) + c // SLAB) * RES_STRIDE, tb), :] for c in range(ROWS)]
    y = h_ref[...] + jnp.concatenate(planes, axis=-1)
    y = y * lax.rsqrt(jnp.mean(y * y, axis=-1, keepdims=True) + RMS_EPS)
    o_ref[...] = y * nw_ref[...]


def _peer_u(idx_rows, tab, xt, g):
    tb = 2 * IDX_TOKENS
    n = g.shape[0]
    return pl.pallas_call(
        _peer_u_kernel,
        grid=(n // tb,),
        in_specs=[pl.BlockSpec(memory_space=pl.ANY),
                  pl.BlockSpec(memory_space=pltpu.VMEM),
                  pl.BlockSpec((tb * ROWS, LANES), lambda i: (i, 0)),
                  pl.BlockSpec((tb, PEER_SEL), lambda i: (i, 0))],
        out_specs=pl.BlockSpec((tb, GATHER_COLS), lambda i: (i, 0)),
        out_shape=jax.ShapeDtypeStruct((n, GATHER_COLS), jnp.float32),
        scratch_shapes=[pltpu.SMEM((2, IDX_TOKENS * PEER_SEL), jnp.int32),
                        pltpu.SemaphoreType.DMA((2,)),
                        pltpu.VMEM((tb, GATHER_COLS), jnp.float32)],
        compiler_params=pltpu.CompilerParams(dimension_semantics=("arbitrary",), vmem_limit_bytes=PEER_VMEM),
        name="peer_u",
    )(idx_rows, tab, xt, g)


def _peer_v(idx_rows, tab, w, h, nw):
    tb = 2 * IDX_TOKENS
    n = h.shape[0]
    return pl.pallas_call(
        _peer_v_kernel,
        grid=(n // tb,),
        in_specs=[pl.BlockSpec(memory_space=pl.ANY),
                  pl.BlockSpec(memory_space=pltpu.VMEM),
                  pl.BlockSpec((tb, GATHER_COLS), lambda i: (i, 0)),
                  pl.BlockSpec((tb, D_MODEL), lambda i: (i, 0)),
                  pl.BlockSpec((1, D_MODEL), lambda i: (0, 0))],
        out_specs=pl.BlockSpec((tb, D_MODEL), lambda i: (i, 0)),
        out_shape=jax.ShapeDtypeStruct((n, D_MODEL), jnp.float32),
        scratch_shapes=[pltpu.SMEM((2, IDX_TOKENS * PEER_SEL), jnp.int32),
                        pltpu.SemaphoreType.DMA((2,)),
                        pltpu.VMEM((ROWS * RES_STRIDE, LANES), jnp.float32)],
        compiler_params=pltpu.CompilerParams(dimension_semantics=("arbitrary",), vmem_limit_bytes=PEER_VMEM),
        name="peer_v",
    )(idx_rows, tab, w, h, nw)


def _peer_block(xt, n2b, h, w_query, sub_keys, u, v, norm_final_w):
    n = h.shape[0]
    idx, g = _peer_topk(n2b, w_query.T.astype(jnp.bfloat16), sub_keys.astype(jnp.bfloat16))
    idx_rows = idx.reshape(n // IDX_TOKENS, IDX_TOKENS * PEER_SEL)
    w = _peer_u(idx_rows, _pack_table(u), xt, g)
    return _peer_v(idx_rows, _pack_table(v), w, h, norm_final_w.reshape(1, D_MODEL))


def kernel(x, norm_mix_w, w_in, hgrn_lb_param, hgrn_norm_w, diff_lambda, diff_norm_w,
           w_out, norm_ffn_w, peer_w_query, peer_sub_keys, peer_u, peer_v, norm_final_w):
    bsz, seq, d = x.shape
    n = bsz * seq
    x2 = x.reshape(n, d)
    proj_hg, proj_da = _in_proj(x2, norm_mix_w[0], w_in[0].astype(jnp.bfloat16))
    hg_out = _hgrn2_pallas(proj_hg.reshape(bsz, seq, -1), hgrn_lb_param, hgrn_norm_w[0])
    proj_da = proj_da.reshape(bsz, seq, -1)
    da_out = _diff_attn_pallas(proj_da, proj_da, proj_da, diff_lambda[0], diff_norm_w[0],
                               cols=(0, DA_HEADS, 2 * DA_HEADS))
    h, xt, n2b = _out_proj(hg_out.reshape(n, -1), da_out.reshape(n, -1), w_out[0].astype(jnp.bfloat16),
                           x2, norm_ffn_w[0])
    out = _peer_block(xt, n2b, h, peer_w_query[0], peer_sub_keys[0], peer_u[0], peer_v[0], norm_final_w)
    return out.reshape(bsz, seq, d)
```
